```python
import math
import jax, jax.numpy as jnp
from jax import lax
import numpy as np

D_MODEL = 1024
BATCH = 2
SEQ = 8192
DEPTH = 2
DEC_BATCH = 4
DEC_SEQ = 4096
PAST_LEN = 128

HEAD_DIM = 64
A_HEADS = 8
A_KV_HEADS = 2
WINDOW = 128
B_HEADS = 8
B_Q_RANK = 512
B_KV_RANK = 256
B_NOPE = 64
B_ROPE = 32
B_V_DIM = 64
B_QK_DIM = B_NOPE + B_ROPE
C_HEADS = 4
C_V_DIM = 2 * HEAD_DIM
D_HEADS = 8
D_KV_HEADS = 2
D_FF = 2816
GRID_W = 64
ROPE_THETA = 10000.0
NORM_EPS = 1e-6
Q_BLOCK = 128
NEG_INF = -1e30
N_EVEN_LAYERS = (DEPTH + 1) // 2
N_ODD_LAYERS = DEPTH // 2
EV_IN_SIZES = (A_HEADS * HEAD_DIM, A_KV_HEADS * HEAD_DIM, A_KV_HEADS * HEAD_DIM, B_Q_RANK, B_KV_RANK, B_ROPE)
OD_IN_SIZES = (2 * C_HEADS * HEAD_DIM, 2 * C_HEADS * HEAD_DIM, C_HEADS * C_V_DIM, D_HEADS * HEAD_DIM, D_KV_HEADS * HEAD_DIM, D_KV_HEADS * HEAD_DIM)
EV_IN = sum(EV_IN_SIZES)
OD_IN = sum(OD_IN_SIZES)
EV_MIX = A_HEADS * HEAD_DIM + B_HEADS * B_V_DIM
OD_MIX = C_HEADS * C_V_DIM + D_HEADS * HEAD_DIM

kernel_name = 'hybrid_bidir_encoder_swa_mla_diff_axial'


def rms_norm(x, g):
    xf = x.astype(jnp.float32)
    y = xf * lax.rsqrt(jnp.mean(xf * xf, axis=-1, keepdims=True) + NORM_EPS)
    return (y * g.astype(jnp.float32)).astype(x.dtype)


def rope_angles(pos, dim):
    inv = ROPE_THETA ** (-(jnp.arange(0, dim, 2, dtype=jnp.float32) / dim))
    ang = pos.astype(jnp.float32)[:, None] * inv[None, :]
    return jnp.cos(ang), jnp.sin(ang)


def apply_rope(x, cos, sin):
    x1, x2 = jnp.split(x, 2, axis=-1)
    c = cos[:, None, :].astype(x.dtype)
    s = sin[:, None, :].astype(x.dtype)
    return jnp.concatenate([x1 * c - x2 * s, x1 * s + x2 * c], axis=-1)


def apply_axial_rope(x, rope_row, rope_col):
    xr, xc = jnp.split(x, 2, axis=-1)
    return jnp.concatenate([apply_rope(xr, *rope_row), apply_rope(xc, *rope_col)], axis=-1)


def _split_cols(z, sizes):
    offs = np.cumsum(sizes)[:-1].tolist()
    return jnp.split(z, offs, axis=-1)


def _query_blocks(q):
    b, s = q.shape[0], q.shape[1]
    return jnp.moveaxis(q.reshape((b, s // Q_BLOCK, Q_BLOCK) + q.shape[2:]), 1, 0)


def _merge_blocks(o):
    o = jnp.moveaxis(o, 0, 1)
    return o.reshape((o.shape[0], o.shape[1] * o.shape[2]) + o.shape[3:])


def swiglu_ffn(x, g, w_in, w_out):
    gate, up = jnp.split(rms_norm(x, g) @ w_in, 2, axis=-1)
    return (jax.nn.silu(gate) * up) @ w_out


def blocked_attention(q, k, v, scale):
    def one_block(qb):
        sc = jnp.einsum('bqkgd,bjkd->bkgqj', qb, k).astype(jnp.float32) * scale
        pr = jax.nn.softmax(sc, axis=-1).astype(v.dtype)
        return jnp.einsum('bkgqj,bjke->bqkge', pr, v)
    o = _merge_blocks(lax.map(one_block, _query_blocks(q)))
    return o.reshape(o.shape[0], o.shape[1], -1)


def sliding_window_attention_with_sink(q, k, v, sink):
    b, s, hq, d = q.shape
    hk = k.shape[2]
    g = hq // hk
    nb = s // WINDOW
    qb = q.reshape(b, nb, WINDOW, hk, g, d)
    pad = ((0, 0), (WINDOW, WINDOW), (0, 0), (0, 0))

    def band(t):
        tb = jnp.pad(t, pad).reshape(b, nb + 2, WINDOW, hk, d)
        return jnp.concatenate([tb[:, :-2], tb[:, 1:-1], tb[:, 2:]], axis=2)

    kb, vb = band(k), band(v)
    qi = jnp.arange(WINDOW)[:, None]
    kj = jnp.arange(3 * WINDOW)[None, :]
    rel = kj - WINDOW - qi
    kpos = jnp.arange(nb)[:, None] * WINDOW - WINDOW + jnp.arange(3 * WINDOW)[None, :]
    valid = (jnp.abs(rel) <= WINDOW)[None] & ((kpos >= 0) & (kpos < s))[:, None, :]
    sc = jnp.einsum('bnqkgd,bnjkd->bnkgqj', qb, kb).astype(jnp.float32) * (d ** -0.5)
    sc = jnp.where(valid[None, :, None, None], sc, NEG_INF)
    sink_l = jnp.broadcast_to(sink.astype(jnp.float32).reshape(1, 1, hk, g, 1, 1), sc.shape[:-1] + (1,))
    pr = jax.nn.softmax(jnp.concatenate([sc, sink_l], axis=-1), axis=-1)[..., :-1]
    o = jnp.einsum('bnkgqj,bnjkd->bnqkgd', pr.astype(v.dtype), vb)
    return o.reshape(b, s, hq * d)


def differential_attention(q, k, v, lam):
    scale = HEAD_DIM ** -0.5

    def one_block(qb):
        sc = jnp.einsum('bqhcd,bjhcd->bhcqj', qb, k).astype(jnp.float32) * scale
        pm = jax.nn.softmax(sc, axis=-1)
        diff = pm[:, :, 0] - lam * pm[:, :, 1]
        return jnp.einsum('bhqj,bjhe->bqhe', diff.astype(v.dtype), v)
    return _merge_blocks(lax.map(one_block, _query_blocks(q)))


def even_mixer(h, p, i, rope_full, rope_mla):
    b, s, _ = h.shape
    a_q, a_k, a_v, b_cq, b_ckv, b_kr = _split_cols(h @ p['ev_w_in'][i], EV_IN_SIZES)
    qa = apply_rope(rms_norm(a_q.reshape(b, s, A_HEADS, HEAD_DIM), p['a_q_norm'][i]), *rope_full)
    ka = apply_rope(rms_norm(a_k.reshape(b, s, A_KV_HEADS, HEAD_DIM), p['a_k_norm'][i]), *rope_full)
    va = a_v.reshape(b, s, A_KV_HEADS, HEAD_DIM)
    o_a = sliding_window_attention_with_sink(qa, ka, va, p['a_sink'][i])
    qb = (rms_norm(b_cq, p['b_cq_norm'][i]) @ p['b_w_uq'][i]).reshape(b, s, B_HEADS, B_QK_DIM)
    kv = (rms_norm(b_ckv, p['b_ckv_norm'][i]) @ p['b_w_ukv'][i]).reshape(b, s, B_HEADS, B_NOPE + B_V_DIM)
    k_nope, vb = jnp.split(kv, [B_NOPE], axis=-1)
    k_r = jnp.broadcast_to(b_kr[:, :, None, :], (b, s, B_HEADS, B_ROPE))
    kb = jnp.concatenate([k_nope, k_r], axis=-1)
    qb = rms_norm(qb, p['b_q_norm'][i])
    kb = rms_norm(kb, p['b_k_norm'][i])
    qb = jnp.concatenate([qb[..., :B_NOPE], apply_rope(qb[..., B_NOPE:], *rope_mla)], axis=-1)
    kb = jnp.concatenate([kb[..., :B_NOPE], apply_rope(kb[..., B_NOPE:], *rope_mla)], axis=-1)
    o_b = blocked_attention(qb[:, :, :, None, :], kb, vb, B_QK_DIM ** -0.5)
    return jnp.concatenate([o_a, o_b], axis=-1) @ p['ev_w_out'][i]


def odd_mixer(h, p, i, layer, rope_full, rope_row, rope_col):
    b, s, _ = h.shape
    c_q, c_k, c_v, d_q, d_k, d_v = _split_cols(h @ p['od_w_in'][i], OD_IN_SIZES)
    lam_init = 0.8 - 0.6 * math.exp(-0.3 * layer)
    qc = apply_rope(rms_norm(c_q.reshape(b, s, 2 * C_HEADS, HEAD_DIM), p['c_q_norm'][i]), *rope_full)
    kc = apply_rope(rms_norm(c_k.reshape(b, s, 2 * C_HEADS, HEAD_DIM), p['c_k_norm'][i]), *rope_full)
    qc = qc.reshape(b, s, C_HEADS, 2, HEAD_DIM)
    kc = kc.reshape(b, s, C_HEADS, 2, HEAD_DIM)
    vc = c_v.reshape(b, s, C_HEADS, C_V_DIM)
    lp = p['c_lambda'][i].astype(jnp.float32)
    lam = jnp.exp(jnp.sum(lp[0] * lp[1])) - jnp.exp(jnp.sum(lp[2] * lp[3])) + lam_init
    o_c = differential_attention(qc, kc, vc, lam)
    o_c = (rms_norm(o_c, p['c_out_norm'][i]) * (1.0 - lam_init)).reshape(b, s, C_HEADS * C_V_DIM)
    qd = apply_axial_rope(rms_norm(d_q.reshape(b, s, D_HEADS, HEAD_DIM), p['d_q_norm'][i]), rope_row, rope_col)
    kd = apply_axial_rope(rms_norm(d_k.reshape(b, s, D_KV_HEADS, HEAD_DIM), p['d_k_norm'][i]), rope_row, rope_col)
    vd = d_v.reshape(b, s, D_KV_HEADS, HEAD_DIM)
    o_d = blocked_attention(qd.reshape(b, s, D_KV_HEADS, D_HEADS // D_KV_HEADS, HEAD_DIM), kd, vd, HEAD_DIM ** -0.5)
    return jnp.concatenate([o_c, o_d], axis=-1) @ p['od_w_out'][i]


def encoder_trunk(x, p):
    s = x.shape[1]
    rows = s // GRID_W
    pos = jnp.arange(s)
    row = jnp.repeat(jnp.arange(rows), GRID_W)
    col = jnp.tile(jnp.arange(GRID_W), rows)
    rope_full = rope_angles(pos, HEAD_DIM)
    rope_mla = rope_angles(pos, B_ROPE)
    rope_row = rope_angles(row, HEAD_DIM // 2)
    rope_col = rope_angles(col, HEAD_DIM // 2)
    for l in range(DEPTH):
        x = x + 0.5 * swiglu_ffn(x, p['ffn1_norm'][l], p['ffn1_w_in'][l], p['ffn1_w_out'][l])
        if l % 2 == 0:
            i = l // 2
            x = x + even_mixer(rms_norm(x, p['ev_norm'][i]), p, i, rope_full, rope_mla)
        else:
            i = l // 2
            x = x + odd_mixer(rms_norm(x, p['od_norm'][i]), p, i, l, rope_full, rope_row, rope_col)
        x = x + 0.5 * swiglu_ffn(x, p['ffn2_norm'][l], p['ffn2_w_in'][l], p['ffn2_w_out'][l])
    return x


def setup_inputs(seed: int = 0) -> dict:
    key = jax.random.key(seed)
    ks = iter(jax.random.split(key, 32))

    def nrm(shape, scale):
        return jax.random.normal(next(ks), shape, jnp.float32) * scale

    def gain(shape):
        return 1.0 + 0.05 * jax.random.normal(next(ks), shape, jnp.float32)

    L, E, O = DEPTH, N_EVEN_LAYERS, N_ODD_LAYERS
    return {
        'x_prompt': nrm((BATCH, SEQ, D_MODEL), 1.0),
        'x_sample': nrm((DEC_BATCH, DEC_SEQ, D_MODEL), 1.0),
        'ffn1_norm': gain((L, D_MODEL)),
        'ffn1_w_in': nrm((L, D_MODEL, 2 * D_FF), D_MODEL ** -0.5),
        'ffn1_w_out': nrm((L, D_FF, D_MODEL), D_FF ** -0.5),
        'ffn2_norm': gain((L, D_MODEL)),
        'ffn2_w_in': nrm((L, D_MODEL, 2 * D_FF), D_MODEL ** -0.5),
        'ffn2_w_out': nrm((L, D_FF, D_MODEL), D_FF ** -0.5),
        'ev_norm': gain((E, D_MODEL)),
        'ev_w_in': nrm((E, D_MODEL, EV_IN), D_MODEL ** -0.5),
        'a_q_norm': gain((E, HEAD_DIM)),
        'a_k_norm': gain((E, HEAD_DIM)),
        'a_sink': nrm((E, A_HEADS), 0.5),
        'b_cq_norm': gain((E, B_Q_RANK)),
        'b_w_uq': nrm((E, B_Q_RANK, B_HEADS * B_QK_DIM), B_Q_RANK ** -0.5),
        'b_ckv_norm': gain((E, B_KV_RANK)),
        'b_w_ukv': nrm((E, B_KV_RANK, B_HEADS * (B_NOPE + B_V_DIM)), B_KV_RANK ** -0.5),
        'b_q_norm': gain((E, B_QK_DIM)),
        'b_k_norm': gain((E, B_QK_DIM)),
        'ev_w_out': nrm((E, EV_MIX, D_MODEL), EV_MIX ** -0.5),
        'od_norm': gain((O, D_MODEL)),
        'od_w_in': nrm((O, D_MODEL, OD_IN), D_MODEL ** -0.5),
        'c_q_norm': gain((O, HEAD_DIM)),
        'c_k_norm': gain((O, HEAD_DIM)),
        'c_lambda': nrm((O, 4, HEAD_DIM), 0.1),
        'c_out_norm': gain((O, C_V_DIM)),
        'd_q_norm': gain((O, HEAD_DIM)),
        'd_k_norm': gain((O, HEAD_DIM)),
        'od_w_out': nrm((O, OD_MIX, D_MODEL), OD_MIX ** -0.5),
    }


def reference(x_prompt, x_sample, ffn1_norm, ffn1_w_in, ffn1_w_out, ffn2_norm, ffn2_w_in, ffn2_w_out,
              ev_norm, ev_w_in, a_q_norm, a_k_norm, a_sink, b_cq_norm, b_w_uq, b_ckv_norm, b_w_ukv,
              b_q_norm, b_k_norm, ev_w_out, od_norm, od_w_in, c_q_norm, c_k_norm, c_lambda, c_out_norm,
              d_q_norm, d_k_norm, od_w_out):
    p = {
        'ffn1_norm': ffn1_norm, 'ffn1_w_in': ffn1_w_in, 'ffn1_w_out': ffn1_w_out,
        'ffn2_norm': ffn2_norm, 'ffn2_w_in': ffn2_w_in, 'ffn2_w_out': ffn2_w_out,
        'ev_norm': ev_norm, 'ev_w_in': ev_w_in, 'a_q_norm': a_q_norm, 'a_k_norm': a_k_norm,
        'a_sink': a_sink, 'b_cq_norm': b_cq_norm, 'b_w_uq': b_w_uq, 'b_ckv_norm': b_ckv_norm,
        'b_w_ukv': b_w_ukv, 'b_q_norm': b_q_norm, 'b_k_norm': b_k_norm, 'ev_w_out': ev_w_out,
        'od_norm': od_norm, 'od_w_in': od_w_in, 'c_q_norm': c_q_norm, 'c_k_norm': c_k_norm,
        'c_lambda': c_lambda, 'c_out_norm': c_out_norm, 'd_q_norm': d_q_norm, 'd_k_norm': d_k_norm,
        'od_w_out': od_w_out,
    }
    y_prompt = encoder_trunk(x_prompt, p)
    y_sample = encoder_trunk(x_sample, p)
    return (y_prompt, y_sample)
```

```python
import functools
import math

import jax
import jax.numpy as jnp
from jax import lax
from jax.experimental import pallas as pl
from jax.experimental.pallas import tpu as pltpu

F32 = jnp.float32
BF16 = jnp.bfloat16

D_MODEL = 1024
D_FF = 2816
HEAD_DIM = 64
N_HEADS = 8
N_KV_HEADS = 2
WINDOW = 128
MLA_Q_RANK = 512
MLA_KV_RANK = 256
MLA_NOPE = 64
MLA_ROPE = 32
MLA_QK = MLA_NOPE + MLA_ROPE
DIFF_HEADS = 4
GRID_W = 64
ROPE_THETA = 10000.0
NORM_EPS = 1e-6
NEG_INF = -1e30
LANES = 128

VMEM_LIMIT = 56 * 1024 * 1024


def _cparams(n_axes):
    return pltpu.CompilerParams(
        dimension_semantics=("arbitrary",) * n_axes, vmem_limit_bytes=VMEM_LIMIT)


def _full(shape):
    return pl.BlockSpec(shape, lambda *_: (0,) * len(shape))


def _resident(shape):
    return pl.BlockSpec(shape, lambda *_: (0,) * len(shape), pipeline_mode=pl.Buffered(1))


def _rms(x, g):
    ms = jnp.mean(x * x, axis=-1, keepdims=True)
    return x * lax.rsqrt(ms + NORM_EPS) * g


def _lane_iota(shape):
    return lax.broadcasted_iota(jnp.int32, shape, len(shape) - 1)


def _swiglu_half_step(x, g_ref, win_ref, wout_ref):
    n = _rms(x, g_ref[...]).astype(BF16)
    z = jnp.dot(n, win_ref[...], preferred_element_type=F32)
    gate = z[:, :D_FF]
    up = z[:, D_FF:]
    h = (gate * jax.nn.sigmoid(gate) * up).astype(BF16)
    y = jnp.dot(h, wout_ref[...], preferred_element_type=F32)
    return x + 0.5 * y


def _ffn_body(x_ref, g_ref, win_ref, wout_ref, o_ref):
    o_ref[...] = _swiglu_half_step(x_ref[...], g_ref, win_ref, wout_ref)


def _mix_ffn_body(x_ref, a1_ref, a2_ref, wo_ref, g_ref, win_ref, wout_ref, o_ref):
    half = a1_ref.shape[1]
    x = x_ref[...]
    x = x + jnp.dot(a1_ref[...], wo_ref[:half, :], preferred_element_type=F32)
    x = x + jnp.dot(a2_ref[...], wo_ref[half:, :], preferred_element_type=F32)
    o_ref[...] = _swiglu_half_step(x, g_ref, win_ref, wout_ref)


def _ffn(x, g, win, wout, tm):
    n = x.shape[0]
    row = pl.BlockSpec((tm, D_MODEL), lambda i: (i, 0))
    return pl.pallas_call(
        _ffn_body,
        grid=(n // tm,),
        in_specs=[row, _full((1, D_MODEL)), _resident(win.shape), _resident(wout.shape)],
        out_specs=row,
        out_shape=jax.ShapeDtypeStruct(x.shape, F32),
        compiler_params=_cparams(1),
        name="ffn",
    )(x, g, win, wout)


def _mix_ffn(x, a1, a2, wo, g, win, wout, tm):
    n = x.shape[0]
    row = pl.BlockSpec((tm, D_MODEL), lambda i: (i, 0))
    arow = pl.BlockSpec((tm, a1.shape[1]), lambda i: (i, 0))
    return pl.pallas_call(
        _mix_ffn_body,
        grid=(n // tm,),
        in_specs=[row, arow, arow, _resident(wo.shape), _full((1, D_MODEL)),
                  _resident(win.shape), _resident(wout.shape)],
        out_specs=row,
        out_shape=jax.ShapeDtypeStruct(x.shape, F32),
        compiler_params=_cparams(1),
        name="mix_ffn",
    )(x, a1, a2, wo, g, win, wout)


def _head_norm(z, gain, dim, halves):
    zz = z * z
    if halves:
        lo = _lane_iota(z.shape) < HEAD_DIM
        s0 = jnp.sum(jnp.where(lo, zz, 0.0), axis=-1, keepdims=True)
        s1 = jnp.sum(jnp.where(lo, 0.0, zz), axis=-1, keepdims=True)
        r = jnp.where(lo, lax.rsqrt(s0 * (1.0 / dim) + NORM_EPS),
                      lax.rsqrt(s1 * (1.0 / dim) + NORM_EPS))
    else:
        r = lax.rsqrt(jnp.sum(zz, axis=-1, keepdims=True) * (1.0 / dim) + NORM_EPS)
    return z * r * gain


def _rope(y, tabs, shift):
    cos, sin_up, sin_dn = tabs
    return (y * cos + pltpu.roll(y, LANES - shift, 1) * sin_up
            + pltpu.roll(y, shift, 1) * sin_dn)


def _load_tabs(refs):
    return tuple(r[...] for r in refs)


def _proj_even_body(x_ref, g_ref, win_ref, wuq_ref, wukv_ref,
                    gaq_ref, gak_ref, gcq_ref, gckv_ref, gbq_ref, gbk_ref,
                    fc_ref, fu_ref, fd_ref, mc_ref, mu_ref, md_ref,
                    qa_ref, ka_ref, va_ref, qb_ref, kb_ref, vb_ref):
    full = _load_tabs((fc_ref, fu_ref, fd_ref))
    mla = _load_tabs((mc_ref, mu_ref, md_ref))
    h = _rms(x_ref[...], g_ref[...]).astype(BF16)
    z = jnp.dot(h, win_ref[...], preferred_element_type=F32)
    o = 0
    qa = z[:, o:o + N_HEADS * LANES]; o += N_HEADS * LANES
    ka = z[:, o:o + LANES]; o += LANES
    va = z[:, o:o + 2 * LANES]; o += 2 * LANES
    cq = z[:, o:o + MLA_Q_RANK]; o += MLA_Q_RANK
    ckv = z[:, o:o + MLA_KV_RANK]; o += MLA_KV_RANK
    kr = z[:, o:o + LANES]

    a_scale = HEAD_DIM ** -0.5
    for hd in range(N_HEADS):
        blk = slice(hd * LANES, (hd + 1) * LANES)
        y = _head_norm(qa[:, blk], gaq_ref[...], HEAD_DIM, False)
        qa_ref[:, blk] = (_rope(y, full, HEAD_DIM // 2) * a_scale).astype(BF16)
    y = _head_norm(ka, gak_ref[...], HEAD_DIM, True)
    ka_ref[...] = _rope(y, full, HEAD_DIM // 2).astype(BF16)
    va_ref[...] = va.astype(BF16)

    cqn = _rms(cq, gcq_ref[...]).astype(BF16)
    qb = jnp.dot(cqn, wuq_ref[...], preferred_element_type=F32)
    ckvn = _rms(ckv, gckv_ref[...]).astype(BF16)
    kv = jnp.dot(ckvn, wukv_ref[...], preferred_element_type=F32)
    vb_ref[...] = kv[:, N_HEADS * LANES:].astype(BF16)
    b_scale = MLA_QK ** -0.5
    for hd in range(N_HEADS):
        blk = slice(hd * LANES, (hd + 1) * LANES)
        y = _head_norm(qb[:, blk], gbq_ref[...], MLA_QK, False)
        qb_ref[:, blk] = (_rope(y, mla, MLA_ROPE // 2) * b_scale).astype(BF16)
        y = _head_norm(kv[:, blk] + kr, gbk_ref[...], MLA_QK, False)
        kb_ref[:, blk] = _rope(y, mla, MLA_ROPE // 2).astype(BF16)


def _proj_odd_body(x_ref, g_ref, win_ref,
                   gcq_ref, gck_ref, gdq_ref, gdk_ref,
                   fc_ref, fu_ref, fd_ref, xc_ref, xu_ref, xd_ref,
                   qc_ref, kc_ref, vc_ref, qd_ref, kd_ref, vd_ref):
    full = _load_tabs((fc_ref, fu_ref, fd_ref))
    axial = _load_tabs((xc_ref, xu_ref, xd_ref))
    h = _rms(x_ref[...], g_ref[...]).astype(BF16)
    z = jnp.dot(h, win_ref[...], preferred_element_type=F32)
    wc = DIFF_HEADS * LANES
    o = 0
    qc = z[:, o:o + wc]; o += wc
    kc = z[:, o:o + wc]; o += wc
    vc = z[:, o:o + wc]; o += wc
    qd = z[:, o:o + N_HEADS * LANES]; o += N_HEADS * LANES
    kd = z[:, o:o + LANES]; o += LANES
    vd = z[:, o:o + 2 * LANES]

    scale = HEAD_DIM ** -0.5
    for hd in range(DIFF_HEADS):
        blk = slice(hd * LANES, (hd + 1) * LANES)
        y = _head_norm(qc[:, blk], gcq_ref[...], HEAD_DIM, True)
        qc_ref[:, blk] = (_rope(y, full, HEAD_DIM // 2) * scale).astype(BF16)
        y = _head_norm(kc[:, blk], gck_ref[...], HEAD_DIM, True)
        kc_ref[:, blk] = _rope(y, full, HEAD_DIM // 2).astype(BF16)
    vc_ref[...] = vc.astype(BF16)
    for hd in range(N_HEADS):
        blk = slice(hd * LANES, (hd + 1) * LANES)
        y = _head_norm(qd[:, blk], gdq_ref[...], HEAD_DIM, False)
        qd_ref[:, blk] = (_rope(y, axial, HEAD_DIM // 4) * scale).astype(BF16)
    y = _head_norm(kd, gdk_ref[...], HEAD_DIM, True)
    kd_ref[...] = _rope(y, axial, HEAD_DIM // 4).astype(BF16)
    vd_ref[...] = vd.astype(BF16)


def _proj(body, name, x, seq, ts, g, mats, gains, tabs, out_widths):
    n = x.shape[0]
    pos_blocks = seq // ts
    row = lambda w: pl.BlockSpec((ts, w), lambda i: (i, 0))
    tab = pl.BlockSpec((ts, LANES), lambda i: (i % pos_blocks, 0))
    in_specs = ([row(D_MODEL), _full((1, D_MODEL))] + [_resident(m.shape) for m in mats]
                + [_full(g.shape) for g in gains] + [tab] * len(tabs))
    return pl.pallas_call(
        body,
        grid=(n // ts,),
        in_specs=in_specs,
        out_specs=[row(w) for w in out_widths],
        out_shape=[jax.ShapeDtypeStruct((n, w), BF16) for w in out_widths],
        compiler_params=_cparams(1),
        name=name,
    )(x, g, *mats, *gains, *tabs)


def _softmax_pv(q, k, v):
    s = lax.dot_general(q, k, (((1,), (1,)), ((), ())), preferred_element_type=F32)
    m = jnp.max(s, axis=-1, keepdims=True)
    p = jnp.exp(s - m)
    den = jnp.sum(p, axis=-1, keepdims=True)
    pv = jnp.dot(p.astype(BF16), v, preferred_element_type=F32)
    return pv / den


def _pair_attn_body(q_ref, k_ref, v_ref, o_ref, *, k_per_head):
    outs = []
    for e in range(2):
        q = q_ref[:, e * LANES:(e + 1) * LANES]
        k = k_ref[:, e * LANES:(e + 1) * LANES] if k_per_head else k_ref[...]
        outs.append(_softmax_pv(q, k, v_ref[...]))
    lo = _lane_iota(outs[0].shape) < HEAD_DIM
    o_ref[...] = jnp.where(lo, outs[0], outs[1]).astype(BF16)


def _pair_attn(q, k, v, batch, seq, tq, *, k_per_head, v_per_pair, name):
    n = q.shape[0]
    qb = seq // tq
    pairs = N_HEADS // 2
    kw = 2 * LANES if k_per_head else LANES
    k_idx = (lambda b, j, i: (b, j)) if k_per_head else (lambda b, j, i: (b, 0))
    if v_per_pair:
        v_idx = lambda b, j, i: (b, j)
    else:
        v_idx = lambda b, j, i: (b, j // (pairs // N_KV_HEADS))
    return pl.pallas_call(
        functools.partial(_pair_attn_body, k_per_head=k_per_head),
        grid=(batch, pairs, qb),
        in_specs=[pl.BlockSpec((tq, 2 * LANES), lambda b, j, i: (b * qb + i, j)),
                  pl.BlockSpec((seq, kw), k_idx),
                  pl.BlockSpec((seq, LANES), v_idx)],
        out_specs=pl.BlockSpec((tq, LANES), lambda b, j, i: (b * qb + i, j)),
        out_shape=jax.ShapeDtypeStruct((n, pairs * LANES), BF16),
        compiler_params=_cparams(3),
        name=name,
    )(q, k, v)


def _diff_attn_body(q_ref, k_ref, v_ref, lam_ref, g_ref, o_ref, *, lam_init):
    tq = q_ref.shape[0]
    lp = lam_ref[...]
    lam = (jnp.exp(jnp.sum(lp[0:1] * lp[1:2], keepdims=True))
           - jnp.exp(jnp.sum(lp[2:3] * lp[3:4], keepdims=True)) + lam_init)
    q = q_ref[...]
    lo = _lane_iota(q.shape) < HEAD_DIM
    zero = jnp.zeros_like(q)
    q2 = jnp.concatenate([jnp.where(lo, q, zero), jnp.where(lo, zero, q)], axis=0)
    s = lax.dot_general(q2, k_ref[...], (((1,), (1,)), ((), ())), preferred_element_type=F32)
    m = jnp.max(s, axis=-1, keepdims=True)
    p = jnp.exp(s - m)
    pn = p / jnp.sum(p, axis=-1, keepdims=True)
    diff = pn[:tq] - lam * pn[tq:]
    o = jnp.dot(diff.astype(BF16), v_ref[...], preferred_element_type=F32)
    o_ref[...] = (_rms(o, g_ref[...]) * (1.0 - lam_init)).astype(BF16)


def _diff_attn(q, k, v, lam_p, gain, batch, seq, tq, lam_init):
    n = q.shape[0]
    qb = seq // tq
    return pl.pallas_call(
        functools.partial(_diff_attn_body, lam_init=lam_init),
        grid=(batch, DIFF_HEADS, qb),
        in_specs=[pl.BlockSpec((tq, LANES), lambda b, h, i: (b * qb + i, h)),
                  pl.BlockSpec((seq, LANES), lambda b, h, i: (b, h)),
                  pl.BlockSpec((seq, LANES), lambda b, h, i: (b, h)),
                  _full(lam_p.shape), _full(gain.shape)],
        out_specs=pl.BlockSpec((tq, LANES), lambda b, h, i: (b * qb + i, h)),
        out_shape=jax.ShapeDtypeStruct((n, DIFF_HEADS * LANES), BF16),
        compiler_params=_cparams(3),
        name="diff_attn",
    )(q, k, v, lam_p, gain)


def _window_attn_body(sink_ref, q_ref, kp_ref, kc_ref, kn_ref, vp_ref, vc_ref, vn_ref, o_ref,
                      *, seq):
    tq = q_ref.shape[0]
    i = pl.program_id(1)
    k = jnp.concatenate([kp_ref[...], kc_ref[...], kn_ref[...]], axis=0)
    v = jnp.concatenate([vp_ref[...], vc_ref[...], vn_ref[...]], axis=0)
    nk = tq + 2 * WINDOW
    qpos = i * tq + lax.broadcasted_iota(jnp.int32, (tq, nk), 0)
    kpos = i * tq - WINDOW + lax.broadcasted_iota(jnp.int32, (tq, nk), 1)
    valid = (jnp.abs(kpos - qpos) <= WINDOW) & (kpos >= 0) & (kpos < seq)
    heads_per_kv = N_HEADS // N_KV_HEADS
    outs = []
    for hd in range(N_HEADS):
        q = q_ref[:, hd * LANES:(hd + 1) * LANES]
        s = lax.dot_general(q, k, (((1,), (1,)), ((), ())), preferred_element_type=F32)
        s = jnp.where(valid, s, NEG_INF)
        sink = sink_ref[hd]
        m = jnp.maximum(jnp.max(s, axis=-1, keepdims=True), sink)
        p = jnp.exp(s - m)
        den = jnp.sum(p, axis=-1, keepdims=True) + jnp.exp(sink - m)
        pv = jnp.dot(p.astype(BF16), v, preferred_element_type=F32)
        kv = hd // heads_per_kv
        outs.append(pv[:, kv * LANES:(kv + 1) * LANES] / den)
    lo = _lane_iota(outs[0].shape) < HEAD_DIM
    for j in range(N_HEADS // 2):
        o_ref[:, j * LANES:(j + 1) * LANES] = jnp.where(lo, outs[2 * j], outs[2 * j + 1]).astype(BF16)


def _window_attn(sink, q, k, v, batch, seq, tq):
    n = q.shape[0]
    qb = seq // tq
    r = tq // WINDOW
    last = n // WINDOW - 1
    prev = lambda b, i: (jnp.maximum((b * qb + i) * r - 1, 0), 0)
    cur = lambda b, i: (b * qb + i, 0)
    nxt = lambda b, i: (jnp.minimum((b * qb + i + 1) * r, last), 0)
    kw, vw = k.shape[1], v.shape[1]
    return pl.pallas_call(
        functools.partial(_window_attn_body, seq=seq),
        grid=(batch, qb),
        in_specs=[pl.BlockSpec(memory_space=pltpu.SMEM),
                  pl.BlockSpec((tq, N_HEADS * LANES), cur),
                  pl.BlockSpec((WINDOW, kw), prev), pl.BlockSpec((tq, kw), cur),
                  pl.BlockSpec((WINDOW, kw), nxt),
                  pl.BlockSpec((WINDOW, vw), prev), pl.BlockSpec((tq, vw), cur),
                  pl.BlockSpec((WINDOW, vw), nxt)],
        out_specs=pl.BlockSpec((tq, N_HEADS // 2 * LANES), cur),
        out_shape=jax.ShapeDtypeStruct((n, N_HEADS // 2 * LANES), BF16),
        compiler_params=_cparams(2),
        name="window_attn",
    )(sink, q, k, k, k, v, v, v)


def _pad_last(a, width):
    return jnp.pad(a, [(0, 0)] * (a.ndim - 1) + [(0, width - a.shape[-1])])


def _q_in_kv_lanes(w):
    d = w.shape[0]
    w = w.reshape(d, N_KV_HEADS, N_HEADS // N_KV_HEADS, HEAD_DIM)
    z = jnp.zeros_like(w[:, 0])
    blocks = [jnp.concatenate([w[:, 0], z], -1), jnp.concatenate([z, w[:, 1]], -1)]
    return jnp.stack(blocks, 1).reshape(d, N_HEADS * LANES)


def _dup_v(w):
    d = w.shape[0]
    w = w.reshape(d, N_KV_HEADS, 1, HEAD_DIM)
    return jnp.broadcast_to(w, (d, N_KV_HEADS, 2, HEAD_DIM)).reshape(d, 2 * LANES)


def _angles(pos, dim):
    inv = ROPE_THETA ** (-(jnp.arange(0, dim, 2, dtype=F32) / dim))
    ang = pos.astype(F32)[:, None] * inv[None, :]
    return jnp.cos(ang), jnp.sin(ang)


def _rope_tables(seq):
    pos = jnp.arange(seq)
    z = lambda w: jnp.zeros((seq, w), F32)
    c, s = _angles(pos, HEAD_DIM)
    full = tuple(jnp.tile(t, (1, 2)) for t in (
        jnp.concatenate([c, c], -1), jnp.concatenate([-s, z(32)], -1),
        jnp.concatenate([z(32), s], -1)))
    cr, sr = _angles(pos // GRID_W, HEAD_DIM // 2)
    cc, sc = _angles(pos % GRID_W, HEAD_DIM // 2)
    axial = tuple(jnp.tile(t, (1, 2)) for t in (
        jnp.concatenate([cr, cr, cc, cc], -1),
        jnp.concatenate([-sr, z(16), -sc, z(16)], -1),
        jnp.concatenate([z(16), sr, z(16), sc], -1)))
    cm, sm = _angles(pos, MLA_ROPE)
    mla = (jnp.concatenate([jnp.ones((seq, MLA_NOPE), F32), cm, cm, z(32)], -1),
           jnp.concatenate([z(MLA_NOPE), -sm, z(48)], -1),
           jnp.concatenate([z(MLA_NOPE + 16), sm, z(32)], -1))
    return full, axial, mla


def _row(v, width=None):
    v = v.astype(F32)
    if width is not None:
        v = _pad_last(v, width)
    return v.reshape(1, -1)


def _even_params(p, i):
    w = p['ev_w_in'][i]
    sizes = (N_HEADS * HEAD_DIM, N_KV_HEADS * HEAD_DIM, N_KV_HEADS * HEAD_DIM,
             MLA_Q_RANK, MLA_KV_RANK, MLA_ROPE)
    offs = [0]
    for s in sizes:
        offs.append(offs[-1] + s)
    a_q, a_k, a_v, b_cq, b_ckv, b_kr = (w[:, offs[j]:offs[j + 1]] for j in range(6))
    kr_blk = jnp.concatenate([jnp.zeros((D_MODEL, MLA_NOPE), F32), b_kr,
                              jnp.zeros((D_MODEL, LANES - MLA_QK), F32)], -1)
    win = jnp.concatenate([_q_in_kv_lanes(a_q), a_k, _dup_v(a_v), b_cq, b_ckv, kr_blk], -1)
    wuq = _pad_last(p['b_w_uq'][i].reshape(MLA_Q_RANK, N_HEADS, MLA_QK), LANES)
    wuq = wuq.reshape(MLA_Q_RANK, N_HEADS * LANES)
    ukv = p['b_w_ukv'][i].reshape(MLA_KV_RANK, N_HEADS, MLA_NOPE + HEAD_DIM)
    k_nope = _pad_last(ukv[..., :MLA_NOPE], LANES).reshape(MLA_KV_RANK, N_HEADS * LANES)
    v_b = ukv[..., MLA_NOPE:].reshape(MLA_KV_RANK, N_HEADS * HEAD_DIM)
    wukv = jnp.concatenate([k_nope, v_b], -1)
    mats = [m.astype(BF16) for m in (win, wuq, wukv)]
    gains = [_row(jnp.tile(p['a_q_norm'][i], 2)), _row(jnp.tile(p['a_k_norm'][i], 2)),
             _row(p['b_cq_norm'][i]), _row(p['b_ckv_norm'][i]),
             _row(p['b_q_norm'][i], LANES), _row(p['b_k_norm'][i], LANES)]
    return _row(p['ev_norm'][i]), mats, gains


def _odd_params(p, i):
    w = p['od_w_in'][i]
    wc = DIFF_HEADS * 2 * HEAD_DIM
    c_q, c_k, c_v = w[:, :wc], w[:, wc:2 * wc], w[:, 2 * wc:3 * wc]
    o = 3 * wc
    d_q = w[:, o:o + N_HEADS * HEAD_DIM]; o += N_HEADS * HEAD_DIM
    d_k = w[:, o:o + N_KV_HEADS * HEAD_DIM]; o += N_KV_HEADS * HEAD_DIM
    d_v = w[:, o:]
    win = jnp.concatenate([c_q, c_k, c_v, _q_in_kv_lanes(d_q), d_k, _dup_v(d_v)], -1)
    gains = [_row(jnp.tile(p['c_q_norm'][i], 2)), _row(jnp.tile(p['c_k_norm'][i], 2)),
             _row(jnp.tile(p['d_q_norm'][i], 2)), _row(jnp.tile(p['d_k_norm'][i], 2))]
    return _row(p['od_norm'][i]), [win.astype(BF16)], gains


def _tiles(batch, seq):
    n = batch * seq
    return dict(tm=min(512, n), ts=min(512, seq), tq=min(256, seq), tq_diff=min(128, seq),
                tq_win=min(256, seq))


def _trunk(x, p, depth):
    batch, seq, _ = x.shape
    t = _tiles(batch, seq)
    x = x.reshape(batch * seq, D_MODEL)
    full, axial, mla = _rope_tables(seq)
    bf = lambda a: a.astype(BF16)
    for l in range(depth):
        i = l // 2
        x = _ffn(x, _row(p['ffn1_norm'][l]), bf(p['ffn1_w_in'][l]), bf(p['ffn1_w_out'][l]), t['tm'])
        if l % 2 == 0:
            g, mats, gains = _even_params(p, i)
            qa, ka, va, qb, kb, vb = _proj(
                _proj_even_body, "proj_even", x, seq, t['ts'], g, mats, gains,
                list(full) + list(mla),
                [N_HEADS * LANES, LANES, 2 * LANES, N_HEADS * LANES, N_HEADS * LANES,
                 N_HEADS * HEAD_DIM])
            o1 = _window_attn(p['a_sink'][i].astype(F32), qa, ka, va, batch, seq, t['tq_win'])
            o2 = _pair_attn(qb, kb, vb, batch, seq, t['tq'], k_per_head=True, v_per_pair=True,
                            name="mla_attn")
            wo = bf(p['ev_w_out'][i])
        else:
            g, mats, gains = _odd_params(p, i)
            qc, kc, vc, qd, kd, vd = _proj(
                _proj_odd_body, "proj_odd", x, seq, t['ts'], g, mats, gains,
                list(full) + list(axial),
                [DIFF_HEADS * LANES] * 3 + [N_HEADS * LANES, LANES, 2 * LANES])
            lam_init = 0.8 - 0.6 * math.exp(-0.3 * l)
            o1 = _diff_attn(qc, kc, vc, p['c_lambda'][i].astype(F32), _row(p['c_out_norm'][i]),
                            batch, seq, t['tq_diff'], lam_init)
            o2 = _pair_attn(qd, kd, vd, batch, seq, t['tq'], k_per_head=False, v_per_pair=False,
                            name="axial_attn")
            wo = bf(p['od_w_out'][i])
        x = _mix_ffn(x, o1, o2, wo, _row(p['ffn2_norm'][l]), bf(p['ffn2_w_in'][l]),
                     bf(p['ffn2_w_out'][l]), t['tm'])
    return x.reshape(batch, seq, D_MODEL)


def kernel(x_prompt, x_sample, ffn1_norm, ffn1_w_in, ffn1_w_out, ffn2_norm, ffn2_w_in, ffn2_w_out, ev_norm, ev_w_in, a_q_norm, a_k_norm, a_sink, b_cq_norm, b_w_uq, b_ckv_norm, b_w_ukv, b_q_norm, b_k_norm, ev_w_out, od_norm, od_w_in, c_q_norm, c_k_norm, c_lambda, c_out_norm, d_q_norm, d_k_norm, od_w_out):
    p = dict(ffn1_norm=ffn1_norm, ffn1_w_in=ffn1_w_in, ffn1_w_out=ffn1_w_out,
             ffn2_norm=ffn2_norm, ffn2_w_in=ffn2_w_in, ffn2_w_out=ffn2_w_out,
             ev_norm=ev_norm, ev_w_in=ev_w_in, a_q_norm=a_q_norm, a_k_norm=a_k_norm,
             a_sink=a_sink, b_cq_norm=b_cq_norm, b_w_uq=b_w_uq, b_ckv_norm=b_ckv_norm,
             b_w_ukv=b_w_ukv, b_q_norm=b_q_norm, b_k_norm=b_k_norm, ev_w_out=ev_w_out,
             od_norm=od_norm, od_w_in=od_w_in, c_q_norm=c_q_norm, c_k_norm=c_k_norm,
             c_lambda=c_lambda, c_out_norm=c_out_norm, d_q_norm=d_q_norm, d_k_norm=d_k_norm,
             od_w_out=od_w_out)
    depth = ffn1_norm.shape[0]
    return (_trunk(x_prompt, p, depth), _trunk(x_sample, p, depth))
```

```python
import functools
import math

import jax
import jax.numpy as jnp
from jax import lax
from jax.experimental import pallas as pl
from jax.experimental.pallas import tpu as pltpu

F32 = jnp.float32
BF16 = jnp.bfloat16

D_MODEL = 1024
D_FF = 2816
HEAD_DIM = 64
N_HEADS = 8
N_KV_HEADS = 2
WINDOW = 128
MLA_Q_RANK = 512
MLA_KV_RANK = 256
MLA_NOPE = 64
MLA_ROPE = 32
MLA_QK = MLA_NOPE + MLA_ROPE
DIFF_HEADS = 4
GRID_W = 64
ROPE_THETA = 10000.0
NORM_EPS = 1e-6
NEG_INF = -1e30
LOG2E = math.log2(math.e)
LANES = 128

VMEM_LIMIT = 56 * 1024 * 1024
ATTN_ROWS = 512
KEY_TILE = 256


def _cparams(n_axes):
    return pltpu.CompilerParams(
        dimension_semantics=("arbitrary",) * n_axes, vmem_limit_bytes=VMEM_LIMIT)


def _full(shape):
    return pl.BlockSpec(shape, lambda *_: (0,) * len(shape))


def _resident(shape):
    return pl.BlockSpec(shape, lambda *_: (0,) * len(shape), pipeline_mode=pl.Buffered(1))


def _rms(x, g):
    ms = jnp.mean(x * x, axis=-1, keepdims=True)
    return x * lax.rsqrt(ms + NORM_EPS) * g


def _lane_iota(shape):
    return lax.broadcasted_iota(jnp.int32, shape, len(shape) - 1)


def _swiglu_half_step(x, g_ref, win_ref, wout_ref):
    n = _rms(x, g_ref[...]).astype(BF16)
    z = jnp.dot(n, win_ref[...], preferred_element_type=F32)
    gate = z[:, :D_FF]
    up = z[:, D_FF:]
    h = (gate * jax.nn.sigmoid(gate) * up).astype(BF16)
    y = jnp.dot(h, wout_ref[...], preferred_element_type=F32)
    return x + 0.5 * y


def _ffn_body(x_ref, g_ref, win_ref, wout_ref, o_ref):
    o_ref[...] = _swiglu_half_step(x_ref[...], g_ref, win_ref, wout_ref)


def _mix_ffn_body(x_ref, a1_ref, a2_ref, wo_ref, g_ref, win_ref, wout_ref, o_ref):
    half = a1_ref.shape[1]
    x = x_ref[...]
    x = x + jnp.dot(a1_ref[...], wo_ref[:half, :], preferred_element_type=F32)
    x = x + jnp.dot(a2_ref[...], wo_ref[half:, :], preferred_element_type=F32)
    o_ref[...] = _swiglu_half_step(x, g_ref, win_ref, wout_ref)


def _ffn(x, g, win, wout, tm):
    n = x.shape[0]
    row = pl.BlockSpec((tm, D_MODEL), lambda i: (i, 0))
    return pl.pallas_call(
        _ffn_body,
        grid=(n // tm,),
        in_specs=[row, _full((1, D_MODEL)), _resident(win.shape), _resident(wout.shape)],
        out_specs=row,
        out_shape=jax.ShapeDtypeStruct(x.shape, F32),
        compiler_params=_cparams(1),
        name="ffn",
    )(x, g, win, wout)


def _mix_ffn(x, a1, a2, wo, g, win, wout, tm):
    n = x.shape[0]
    row = pl.BlockSpec((tm, D_MODEL), lambda i: (i, 0))
    arow = lambda a: pl.BlockSpec((tm, a.shape[1]), lambda i: (i, 0))
    return pl.pallas_call(
        _mix_ffn_body,
        grid=(n // tm,),
        in_specs=[row, arow(a1), arow(a2), _resident(wo.shape), _full((1, D_MODEL)),
                  _resident(win.shape), _resident(wout.shape)],
        out_specs=row,
        out_shape=jax.ShapeDtypeStruct(x.shape, F32),
        compiler_params=_cparams(1),
        name="mix_ffn",
    )(x, a1, a2, wo, g, win, wout)


def _first_of_pair(shape):
    return (_lane_iota(shape) & (HEAD_DIM // 2)) == 0


def _norm_rope(z, gain, tabs, dim, pair):
    cos, sin_signed = tabs
    zz = z * z
    if pair:
        lo = _first_of_pair(z.shape)
        s0 = jnp.sum(jnp.where(lo, zz, 0.0), axis=-1, keepdims=True)
        s1 = jnp.sum(jnp.where(lo, 0.0, zz), axis=-1, keepdims=True)
        r = jnp.where(lo, lax.rsqrt(s0 * (1.0 / dim) + NORM_EPS),
                      lax.rsqrt(s1 * (1.0 / dim) + NORM_EPS))
    else:
        r = lax.rsqrt(jnp.sum(zz, axis=-1, keepdims=True) * (1.0 / dim) + NORM_EPS)
    y = z * r * gain
    return y * cos + pltpu.roll(y, LANES // 2, 1) * sin_signed


def _load_tabs(refs):
    return tuple(r[...] for r in refs)


def _proj_even_body(x_ref, g_ref, win_ref, wuq_ref, wukv_ref,
                    gaq_ref, gak_ref, gcq_ref, gckv_ref, gbq_ref, gbk_ref,
                    fc_ref, fs_ref, mc_ref, ms_ref,
                    qa_ref, ka_ref, va_ref, qb_ref, kb_ref, vb_ref):
    full = _load_tabs((fc_ref, fs_ref))
    mla = _load_tabs((mc_ref, ms_ref))
    h = _rms(x_ref[...], g_ref[...]).astype(BF16)
    z = jnp.dot(h, win_ref[...], preferred_element_type=F32)
    o = 0
    qa = z[:, o:o + N_HEADS * LANES]; o += N_HEADS * LANES
    ka = z[:, o:o + LANES]; o += LANES
    va = z[:, o:o + 2 * LANES]; o += 2 * LANES
    cq = z[:, o:o + MLA_Q_RANK]; o += MLA_Q_RANK
    ckv = z[:, o:o + MLA_KV_RANK]; o += MLA_KV_RANK
    kr = z[:, o:o + LANES]

    for hd in range(N_HEADS):
        blk = slice(hd * LANES, (hd + 1) * LANES)
        qa_ref[:, blk] = _norm_rope(qa[:, blk], gaq_ref[...], full, HEAD_DIM, False).astype(BF16)
    ka_ref[...] = _norm_rope(ka, gak_ref[...], full, HEAD_DIM, True).astype(BF16)
    va_ref[...] = va.astype(BF16)

    cqn = _rms(cq, gcq_ref[...]).astype(BF16)
    qb = jnp.dot(cqn, wuq_ref[...], preferred_element_type=F32)
    ckvn = _rms(ckv, gckv_ref[...]).astype(BF16)
    kv = jnp.dot(ckvn, wukv_ref[...], preferred_element_type=F32)
    vb_ref[...] = kv[:, N_HEADS * LANES:].astype(BF16)
    for hd in range(N_HEADS):
        blk = slice(hd * LANES, (hd + 1) * LANES)
        qb_ref[:, blk] = _norm_rope(qb[:, blk], gbq_ref[...], mla, MLA_QK, False).astype(BF16)
        kb_ref[:, blk] = _norm_rope(kv[:, blk] + kr, gbk_ref[...], mla, MLA_QK,
                                    False).astype(BF16)


def _proj_odd_body(x_ref, g_ref, win_ref,
                   gcq_ref, gck_ref, gdq_ref, gdk_ref,
                   fc_ref, fs_ref, xc_ref, xs_ref,
                   qc_ref, kc_ref, vc_ref, qd_ref, kd_ref, vd_ref):
    full = _load_tabs((fc_ref, fs_ref))
    axial = _load_tabs((xc_ref, xs_ref))
    h = _rms(x_ref[...], g_ref[...]).astype(BF16)
    z = jnp.dot(h, win_ref[...], preferred_element_type=F32)
    wc = DIFF_HEADS * LANES
    o = 0
    qc = z[:, o:o + wc]; o += wc
    kc = z[:, o:o + wc]; o += wc
    vc = z[:, o:o + wc]; o += wc
    qd = z[:, o:o + N_HEADS * LANES]; o += N_HEADS * LANES
    kd = z[:, o:o + LANES]; o += LANES
    vd = z[:, o:o + 2 * LANES]

    for hd in range(DIFF_HEADS):
        blk = slice(hd * LANES, (hd + 1) * LANES)
        qc_ref[:, blk] = _norm_rope(qc[:, blk], gcq_ref[...], full, HEAD_DIM, True).astype(BF16)
        kc_ref[:, blk] = _norm_rope(kc[:, blk], gck_ref[...], full, HEAD_DIM, True).astype(BF16)
    vc_ref[...] = vc.astype(BF16)
    for hd in range(N_HEADS):
        blk = slice(hd * LANES, (hd + 1) * LANES)
        qd_ref[:, blk] = _norm_rope(qd[:, blk], gdq_ref[...], axial, HEAD_DIM, False).astype(BF16)
    kd_ref[...] = _norm_rope(kd, gdk_ref[...], axial, HEAD_DIM, True).astype(BF16)
    vd_ref[...] = vd.astype(BF16)


def _proj(body, name, x, seq, ts, g, mats, gains, tabs, out_widths):
    n = x.shape[0]
    pos_blocks = seq // ts
    row = lambda w: pl.BlockSpec((ts, w), lambda i: (i, 0))
    tab = pl.BlockSpec((ts, LANES), lambda i: (i % pos_blocks, 0))
    in_specs = ([row(D_MODEL), _full((1, D_MODEL))] + [_resident(m.shape) for m in mats]
                + [_full(g.shape) for g in gains] + [tab] * len(tabs))
    return pl.pallas_call(
        body,
        grid=(n // ts,),
        in_specs=in_specs,
        out_specs=[row(w) for w in out_widths],
        out_shape=[jax.ShapeDtypeStruct((n, w), BF16) for w in out_widths],
        compiler_params=_cparams(1),
        name=name,
    )(x, g, *mats, *gains, *tabs)


def _attn_pipeline_step(q, k_ref, v_ref, s_ref, m_ref):
    rows, seq = s_ref.shape

    @pl.when(pl.program_id(0) == 0)
    def _():
        s_ref[...] = jnp.zeros_like(s_ref)
        m_ref[...] = jnp.zeros_like(m_ref)

    m = m_ref[...]
    m = jnp.concatenate([m] * (KEY_TILE // LANES), axis=-1)
    ones = jnp.ones((KEY_TILE, LANES), BF16)
    acc = jnp.zeros((rows, 2 * LANES), F32)
    mx = jnp.full((rows, LANES), NEG_INF, F32)
    for j in range(seq // KEY_TILE):
        keys = slice(j * KEY_TILE, (j + 1) * KEY_TILE)
        s = lax.dot_general(q, k_ref[keys, :], (((1,), (1,)), ((), ())),
                            preferred_element_type=F32)
        p = jnp.exp2(s_ref[:, keys] - m)
        v1 = jnp.concatenate([v_ref[keys, :], ones], axis=-1)
        acc = acc + jnp.dot(p.astype(BF16), v1, preferred_element_type=F32)
        s_ref[:, keys] = s
        for blk in range(KEY_TILE // LANES):
            mx = jnp.maximum(mx, s[:, blk * LANES:(blk + 1) * LANES])
    m_ref[...] = jnp.broadcast_to(jnp.max(mx, axis=-1, keepdims=True), mx.shape)
    return acc[:, :LANES], acc[:, LANES:]


def _head_attn_body(q_ref, k_ref, v_ref, o_ref, *scratch):
    acc, l = _attn_pipeline_step(q_ref[...], k_ref, v_ref, *scratch)
    o_ref[...] = (acc / l).astype(BF16)


def _diff_attn_body(q_ref, k_ref, v_ref, lam_ref, g_ref, o_ref, *scratch, lam_init):
    tq = q_ref.shape[0]
    q = q_ref[...]
    lo = _first_of_pair(q.shape)
    zero = jnp.zeros_like(q)
    q2 = jnp.concatenate([jnp.where(lo, q, zero), jnp.where(lo, zero, q)], axis=0)
    acc, l = _attn_pipeline_step(q2, k_ref, v_ref, *scratch)
    lp = lam_ref[...]
    lam = (jnp.exp(jnp.sum(lp[0:1] * lp[1:2], keepdims=True))
           - jnp.exp(jnp.sum(lp[2:3] * lp[3:4], keepdims=True)) + lam_init)
    o = acc[:tq] / l[:tq] - lam * (acc[tq:] / l[tq:])
    o_ref[...] = (_rms(o, g_ref[...]) * (1.0 - lam_init)).astype(BF16)


def _attn_scratch(rows, seq):
    return [pltpu.VMEM((rows, seq), F32),
            pltpu.VMEM((rows, LANES), F32)]


def _unit(t, n_units, heads, qb, lag):
    u = jnp.clip(t - lag, 0, n_units - 1)
    return u // (qb * heads), (u // qb) % heads, u % qb


def _head_attn(q, k, v, batch, seq, tq, *, k_per_head, heads_per_v, name):
    n = q.shape[0]
    qb = seq // tq
    n_units = batch * N_HEADS * qb
    unit = functools.partial(_unit, n_units=n_units, heads=N_HEADS, qb=qb)

    def q_idx(t):
        b, h, i = unit(t, lag=0)
        return b * qb + i, h

    def k_idx(t):
        b, h, _ = unit(t, lag=0)
        return b, (h if k_per_head else 0)

    def v_idx(t):
        b, h, _ = unit(t, lag=1)
        return b, h // heads_per_v

    def o_idx(t):
        b, h, i = unit(t, lag=1)
        return b * qb + i, h

    return pl.pallas_call(
        _head_attn_body,
        grid=(n_units + 1,),
        in_specs=[pl.BlockSpec((tq, LANES), q_idx), pl.BlockSpec((seq, LANES), k_idx),
                  pl.BlockSpec((seq, LANES), v_idx)],
        out_specs=pl.BlockSpec((tq, LANES), o_idx),
        out_shape=jax.ShapeDtypeStruct((n, N_HEADS * LANES), BF16),
        scratch_shapes=_attn_scratch(tq, seq),
        compiler_params=_cparams(1),
        name=name,
    )(q, k, v)


def _diff_attn(q, k, v, lam_p, gain, batch, seq, tq, lam_init):
    n = q.shape[0]
    qb = seq // tq
    n_units = batch * DIFF_HEADS * qb
    unit = functools.partial(_unit, n_units=n_units, heads=DIFF_HEADS, qb=qb)

    def row_idx(lag):
        def idx(t):
            b, h, i = unit(t, lag=lag)
            return b * qb + i, h
        return idx

    def seq_idx(lag):
        def idx(t):
            b, h, _ = unit(t, lag=lag)
            return b, h
        return idx

    return pl.pallas_call(
        functools.partial(_diff_attn_body, lam_init=lam_init),
        grid=(n_units + 1,),
        in_specs=[pl.BlockSpec((tq, LANES), row_idx(0)), pl.BlockSpec((seq, LANES), seq_idx(0)),
                  pl.BlockSpec((seq, LANES), seq_idx(1)), _full(lam_p.shape), _full(gain.shape)],
        out_specs=pl.BlockSpec((tq, LANES), row_idx(1)),
        out_shape=jax.ShapeDtypeStruct((n, DIFF_HEADS * LANES), BF16),
        scratch_shapes=_attn_scratch(2 * tq, seq),
        compiler_params=_cparams(1),
        name="diff_attn",
    )(q, k, v, lam_p, gain)


def _window_attn_body(sink_ref, q_ref, kp_ref, kc_ref, kn_ref, vp_ref, vc_ref, vn_ref, o_ref,
                      *, seq):
    tq = q_ref.shape[0]
    i = pl.program_id(1)
    k = jnp.concatenate([kp_ref[...], kc_ref[...], kn_ref[...]], axis=0)
    v = jnp.concatenate([vp_ref[...], vc_ref[...], vn_ref[...]], axis=0)
    nk = tq + 2 * WINDOW
    qpos = i * tq + lax.broadcasted_iota(jnp.int32, (tq, nk), 0)
    kpos = i * tq - WINDOW + lax.broadcasted_iota(jnp.int32, (tq, nk), 1)
    valid = (jnp.abs(kpos - qpos) <= WINDOW) & (kpos >= 0) & (kpos < seq)
    heads_per_kv = N_HEADS // N_KV_HEADS
    outs = []
    for hd in range(N_HEADS):
        q = q_ref[:, hd * LANES:(hd + 1) * LANES]
        s = lax.dot_general(q, k, (((1,), (1,)), ((), ())), preferred_element_type=F32)
        s = jnp.where(valid, s, NEG_INF)
        sink = sink_ref[hd] * LOG2E
        m = jnp.maximum(jnp.max(s, axis=-1, keepdims=True), sink)
        p = jnp.exp2(s - m)
        den = jnp.sum(p, axis=-1, keepdims=True) + jnp.exp2(sink - m)
        pv = jnp.dot(p.astype(BF16), v, preferred_element_type=F32)
        kv = hd // heads_per_kv
        outs.append(pv[:, kv * LANES:(kv + 1) * LANES] / den)
    lo = _lane_iota(outs[0].shape) < HEAD_DIM
    for j in range(N_HEADS // 2):
        o_ref[:, j * LANES:(j + 1) * LANES] = jnp.where(lo, outs[2 * j], outs[2 * j + 1]).astype(BF16)


def _window_attn(sink, q, k, v, batch, seq, tq):
    n = q.shape[0]
    qb = seq // tq
    r = tq // WINDOW
    last = n // WINDOW - 1
    prev = lambda b, i: (jnp.maximum((b * qb + i) * r - 1, 0), 0)
    cur = lambda b, i: (b * qb + i, 0)
    nxt = lambda b, i: (jnp.minimum((b * qb + i + 1) * r, last), 0)
    kw, vw = k.shape[1], v.shape[1]
    return pl.pallas_call(
        functools.partial(_window_attn_body, seq=seq),
        grid=(batch, qb),
        in_specs=[pl.BlockSpec(memory_space=pltpu.SMEM),
                  pl.BlockSpec((tq, N_HEADS * LANES), cur),
                  pl.BlockSpec((WINDOW, kw), prev), pl.BlockSpec((tq, kw), cur),
                  pl.BlockSpec((WINDOW, kw), nxt),
                  pl.BlockSpec((WINDOW, vw), prev), pl.BlockSpec((tq, vw), cur),
                  pl.BlockSpec((WINDOW, vw), nxt)],
        out_specs=pl.BlockSpec((tq, N_HEADS // 2 * LANES), cur),
        out_shape=jax.ShapeDtypeStruct((n, N_HEADS // 2 * LANES), BF16),
        compiler_params=_cparams(2),
        name="window_attn",
    )(sink, q, k, k, k, v, v, v)


def _pad_last(a, width):
    return jnp.pad(a, [(0, 0)] * (a.ndim - 1) + [(0, width - a.shape[-1])])


def _pad_first(a, width):
    return jnp.pad(a, [(0, 0)] * (a.ndim - 1) + [(width - a.shape[-1], 0)])


def _pair_layout(w, axial=False):
    lead = w.shape[:-1]
    if axial:
        w = w.reshape(*lead, -1, 2, 2, 2, HEAD_DIM // 4)
        w = jnp.moveaxis(w, -2, -4)
    else:
        w = w.reshape(*lead, -1, 2, 2, HEAD_DIM // 2)
        w = jnp.swapaxes(w, -2, -3)
    return w.reshape(*lead, -1)


def _q_in_kv_lanes(w, axial=False):
    d = w.shape[0]
    w = w.reshape(d, N_KV_HEADS, N_HEADS // N_KV_HEADS, HEAD_DIM)
    z = jnp.zeros_like(w[:, 0])
    blocks = [jnp.concatenate([w[:, 0], z], -1), jnp.concatenate([z, w[:, 1]], -1)]
    return _pair_layout(jnp.stack(blocks, 1).reshape(d, N_HEADS * LANES), axial)


def _mla_layout(w):
    nope, x1, x2 = w[..., :MLA_NOPE], w[..., MLA_NOPE:MLA_NOPE + 16], w[..., MLA_NOPE + 16:]
    z = jnp.zeros(w.shape[:-1] + (LANES - MLA_QK,), w.dtype)
    return jnp.concatenate([x1, nope[..., :48], x2, nope[..., 48:], z], -1)


def _dup_v(w):
    d = w.shape[0]
    w = w.reshape(d, N_KV_HEADS, 1, HEAD_DIM)
    return jnp.broadcast_to(w, (d, N_KV_HEADS, 2, HEAD_DIM)).reshape(d, 2 * LANES)


def _angles(pos, dim):
    inv = ROPE_THETA ** (-(jnp.arange(0, dim, 2, dtype=F32) / dim))
    ang = pos.astype(F32)[:, None] * inv[None, :]
    return jnp.cos(ang), jnp.sin(ang)


def _rope_tables(seq):
    pos = jnp.arange(seq)
    c, s = _angles(pos, HEAD_DIM)
    full = (jnp.tile(c, (1, 4)), jnp.concatenate([-s, -s, s, s], -1))
    cr, sr = _angles(pos // GRID_W, HEAD_DIM // 2)
    cc, sc = _angles(pos % GRID_W, HEAD_DIM // 2)
    axial = (jnp.tile(jnp.concatenate([cr, cc], -1), (1, 4)),
             jnp.concatenate([-sr, -sc, -sr, -sc, sr, sc, sr, sc], -1))
    cm, sm = _angles(pos, MLA_ROPE)
    one, zero = jnp.ones((seq, 48), F32), jnp.zeros((seq, 48), F32)
    mla = (jnp.concatenate([cm, one, cm, one], -1), jnp.concatenate([-sm, zero, sm, zero], -1))
    return full, axial, mla


def _row(v):
    return v.astype(F32).reshape(1, -1)


def _even_params(p, i):
    w = p['ev_w_in'][i]
    sizes = (N_HEADS * HEAD_DIM, N_KV_HEADS * HEAD_DIM, N_KV_HEADS * HEAD_DIM,
             MLA_Q_RANK, MLA_KV_RANK, MLA_ROPE)
    offs = [0]
    for s in sizes:
        offs.append(offs[-1] + s)
    a_q, a_k, a_v, b_cq, b_ckv, b_kr = (w[:, offs[j]:offs[j + 1]] for j in range(6))
    kr_blk = _mla_layout(_pad_first(b_kr, MLA_QK))
    win = jnp.concatenate([_q_in_kv_lanes(a_q), _pair_layout(a_k), _dup_v(a_v), b_cq, b_ckv,
                           kr_blk], -1)
    wuq = _mla_layout(p['b_w_uq'][i].reshape(MLA_Q_RANK, N_HEADS, MLA_QK))
    wuq = wuq.reshape(MLA_Q_RANK, N_HEADS * LANES)
    ukv = p['b_w_ukv'][i].reshape(MLA_KV_RANK, N_HEADS, MLA_NOPE + HEAD_DIM)
    k_nope = _mla_layout(_pad_last(ukv[..., :MLA_NOPE], MLA_QK))
    k_nope = k_nope.reshape(MLA_KV_RANK, N_HEADS * LANES)
    v_b = ukv[..., MLA_NOPE:].reshape(MLA_KV_RANK, N_HEADS * HEAD_DIM)
    wukv = jnp.concatenate([k_nope, v_b], -1)
    mats = [m.astype(BF16) for m in (win, wuq, wukv)]
    a_scale = HEAD_DIM ** -0.5 * LOG2E
    b_scale = MLA_QK ** -0.5 * LOG2E
    pair_gain = lambda g, scale=1.0: _row(_pair_layout(jnp.tile(g, 2)) * scale)
    gains = [pair_gain(p['a_q_norm'][i], a_scale), pair_gain(p['a_k_norm'][i]),
             _row(p['b_cq_norm'][i]), _row(p['b_ckv_norm'][i]),
             _row(_mla_layout(p['b_q_norm'][i]) * b_scale), _row(_mla_layout(p['b_k_norm'][i]))]
    return _row(p['ev_norm'][i]), mats, gains


def _odd_params(p, i):
    w = p['od_w_in'][i]
    wc = DIFF_HEADS * 2 * HEAD_DIM
    c_q, c_k, c_v = w[:, :wc], w[:, wc:2 * wc], w[:, 2 * wc:3 * wc]
    o = 3 * wc
    d_q = w[:, o:o + N_HEADS * HEAD_DIM]; o += N_HEADS * HEAD_DIM
    d_k = w[:, o:o + N_KV_HEADS * HEAD_DIM]; o += N_KV_HEADS * HEAD_DIM
    d_v = w[:, o:]
    win = jnp.concatenate([_pair_layout(c_q), _pair_layout(c_k), c_v,
                           _q_in_kv_lanes(d_q, axial=True), _pair_layout(d_k, axial=True),
                           _dup_v(d_v)], -1)
    scale = HEAD_DIM ** -0.5 * LOG2E
    pair_gain = lambda g, axial, scale=1.0: _row(_pair_layout(jnp.tile(g, 2), axial) * scale)
    gains = [pair_gain(p['c_q_norm'][i], False, scale), pair_gain(p['c_k_norm'][i], False),
             pair_gain(p['d_q_norm'][i], True, scale), pair_gain(p['d_k_norm'][i], True)]
    return _row(p['od_norm'][i]), [win.astype(BF16)], gains


def _tiles(batch, seq):
    n = batch * seq
    rows = min(ATTN_ROWS, seq)
    return dict(tm=min(512, n), ts=min(512, seq), tq=rows, tq_diff=rows // 2,
                tq_win=min(256, seq))


def _wo_for_padded_heads(w_mix, second_half_lanes):
    w = w_mix.reshape(N_HEADS, HEAD_DIM, D_MODEL)
    z = jnp.zeros_like(w)
    lo, hi = jnp.concatenate([w, z], 1), jnp.concatenate([z, w], 1)
    if second_half_lanes is None:
        return lo.reshape(N_HEADS * LANES, D_MODEL)
    sel = jnp.asarray(second_half_lanes).reshape(N_HEADS, 1, 1)
    return jnp.where(sel, hi, lo).reshape(N_HEADS * LANES, D_MODEL)


def _trunk(x, p, depth):
    batch, seq, _ = x.shape
    t = _tiles(batch, seq)
    x = x.reshape(batch * seq, D_MODEL)
    full, axial, mla = _rope_tables(seq)
    bf = lambda a: a.astype(BF16)
    for l in range(depth):
        i = l // 2
        x = _ffn(x, _row(p['ffn1_norm'][l]), bf(p['ffn1_w_in'][l]), bf(p['ffn1_w_out'][l]), t['tm'])
        if l % 2 == 0:
            g, mats, gains = _even_params(p, i)
            qa, ka, va, qb, kb, vb = _proj(
                _proj_even_body, "proj_even", x, seq, t['ts'], g, mats, gains,
                list(full) + list(mla),
                [N_HEADS * LANES, LANES, 2 * LANES, N_HEADS * LANES, N_HEADS * LANES,
                 N_HEADS * HEAD_DIM])
            o1 = _window_attn(p['a_sink'][i].astype(F32), qa, ka, va, batch, seq, t['tq_win'])
            o2 = _head_attn(qb, kb, vb, batch, seq, t['tq'], k_per_head=True,
                            heads_per_v=2, name="mla_attn")
            w = p['ev_w_out'][i]
            half = N_HEADS * HEAD_DIM
            odd_head = [h % 2 == 1 for h in range(N_HEADS)]
            wo = bf(jnp.concatenate([w[:half], _wo_for_padded_heads(w[half:], odd_head)], 0))
        else:
            g, mats, gains = _odd_params(p, i)
            qc, kc, vc, qd, kd, vd = _proj(
                _proj_odd_body, "proj_odd", x, seq, t['ts'], g, mats, gains,
                list(full) + list(axial),
                [DIFF_HEADS * LANES] * 3 + [N_HEADS * LANES, LANES, 2 * LANES])
            lam_init = 0.8 - 0.6 * math.exp(-0.3 * l)
            o1 = _diff_attn(qc, kc, vc, p['c_lambda'][i].astype(F32), _row(p['c_out_norm'][i]),
                            batch, seq, t['tq_diff'], lam_init)
            o2 = _head_attn(qd, kd, vd, batch, seq, t['tq'], k_per_head=False,
                            heads_per_v=N_HEADS // N_KV_HEADS, name="axial_attn")
            w = p['od_w_out'][i]
            half = DIFF_HEADS * 2 * HEAD_DIM
            wo = bf(jnp.concatenate([w[:half], _wo_for_padded_heads(w[half:], None)], 0))
        x = _mix_ffn(x, o1, o2, wo, _row(p['ffn2_norm'][l]), bf(p['ffn2_w_in'][l]),
                     bf(p['ffn2_w_out'][l]), t['tm'])
    return x.reshape(batch, seq, D_MODEL)


def kernel(x_prompt, x_sample, ffn1_norm, ffn1_w_in, ffn1_w_out, ffn2_norm, ffn2_w_in, ffn2_w_out, ev_norm, ev_w_in, a_q_norm, a_k_norm, a_sink, b_cq_norm, b_w_uq, b_ckv_norm, b_w_ukv, b_q_norm, b_k_norm, ev_w_out, od_norm, od_w_in, c_q_norm, c_k_norm, c_lambda, c_out_norm, d_q_norm, d_k_norm, od_w_out):
    p = dict(ffn1_norm=ffn1_norm, ffn1_w_in=ffn1_w_in, ffn1_w_out=ffn1_w_out,
             ffn2_norm=ffn2_norm, ffn2_w_in=ffn2_w_in, ffn2_w_out=ffn2_w_out,
             ev_norm=ev_norm, ev_w_in=ev_w_in, a_q_norm=a_q_norm, a_k_norm=a_k_norm,
             a_sink=a_sink, b_cq_norm=b_cq_norm, b_w_uq=b_w_uq, b_ckv_norm=b_ckv_norm,
             b_w_ukv=b_w_ukv, b_q_norm=b_q_norm, b_k_norm=b_k_norm, ev_w_out=ev_w_out,
             od_norm=od_norm, od_w_in=od_w_in, c_q_norm=c_q_norm, c_k_norm=c_k_norm,
             c_lambda=c_lambda, c_out_norm=c_out_norm, d_q_norm=d_q_norm, d_k_norm=d_k_norm,
             od_w_out=od_w_out)
    depth = ffn1_norm.shape[0]
    return (_trunk(x_prompt, p, depth), _trunk(x_sample, p, depth))
```

```python
import functools
import math

import jax
import jax.numpy as jnp
from jax import lax
from jax.experimental import pallas as pl
from jax.experimental.pallas import tpu as pltpu

F32 = jnp.float32
BF16 = jnp.bfloat16

D_MODEL = 1024
D_FF = 2816
HEAD_DIM = 64
N_HEADS = 8
N_KV_HEADS = 2
WINDOW = 128
MLA_Q_RANK = 512
MLA_KV_RANK = 256
MLA_NOPE = 64
MLA_ROPE = 32
MLA_QK = MLA_NOPE + MLA_ROPE
DIFF_HEADS = 4
GRID_W = 64
ROPE_THETA = 10000.0
NORM_EPS = 1e-6
NEG_INF = -1e30
LOG2E = math.log2(math.e)
LANES = 128

VMEM_LIMIT = 56 * 1024 * 1024
ATTN_SCORES = 1024 * 8192
KEY_TILE = 256


def _cparams(n_axes):
    return pltpu.CompilerParams(
        dimension_semantics=("arbitrary",) * n_axes, vmem_limit_bytes=VMEM_LIMIT)


def _full(shape):
    return pl.BlockSpec(shape, lambda *_: (0,) * len(shape))


def _resident(shape):
    return pl.BlockSpec(shape, lambda *_: (0,) * len(shape), pipeline_mode=pl.Buffered(1))


def _rms(x, g):
    ms = jnp.mean(x * x, axis=-1, keepdims=True)
    return x * lax.rsqrt(ms + NORM_EPS) * g


def _lane_iota(shape):
    return lax.broadcasted_iota(jnp.int32, shape, len(shape) - 1)


def _swiglu_half_step(x, g_ref, win_ref, wout_ref):
    n = _rms(x, g_ref[...]).astype(BF16)
    z = jnp.dot(n, win_ref[...], preferred_element_type=F32)
    gate = z[:, :D_FF]
    up = z[:, D_FF:]
    h = (gate * jax.nn.sigmoid(gate) * up).astype(BF16)
    y = jnp.dot(h, wout_ref[...], preferred_element_type=F32)
    return x + 0.5 * y


def _ffn_body(x_ref, g_ref, win_ref, wout_ref, o_ref):
    o_ref[...] = _swiglu_half_step(x_ref[...], g_ref, win_ref, wout_ref)


def _mix_ffn_body(x_ref, a1_ref, a2_ref, wo_ref, g_ref, win_ref, wout_ref, o_ref):
    half = a1_ref.shape[1]
    x = x_ref[...]
    x = x + jnp.dot(a1_ref[...], wo_ref[:half, :], preferred_element_type=F32)
    x = x + jnp.dot(a2_ref[...], wo_ref[half:, :], preferred_element_type=F32)
    o_ref[...] = _swiglu_half_step(x, g_ref, win_ref, wout_ref)


def _ffn(x, g, win, wout, tm):
    n = x.shape[0]
    row = pl.BlockSpec((tm, D_MODEL), lambda i: (i, 0))
    return pl.pallas_call(
        _ffn_body,
        grid=(n // tm,),
        in_specs=[row, _full((1, D_MODEL)), _resident(win.shape), _resident(wout.shape)],
        out_specs=row,
        out_shape=jax.ShapeDtypeStruct(x.shape, F32),
        compiler_params=_cparams(1),
        name="ffn",
    )(x, g, win, wout)


def _mix_ffn(x, a1, a2, wo, g, win, wout, tm):
    n = x.shape[0]
    row = pl.BlockSpec((tm, D_MODEL), lambda i: (i, 0))
    arow = lambda a: pl.BlockSpec((tm, a.shape[1]), lambda i: (i, 0))
    return pl.pallas_call(
        _mix_ffn_body,
        grid=(n // tm,),
        in_specs=[row, arow(a1), arow(a2), _resident(wo.shape), _full((1, D_MODEL)),
                  _resident(win.shape), _resident(wout.shape)],
        out_specs=row,
        out_shape=jax.ShapeDtypeStruct(x.shape, F32),
        compiler_params=_cparams(1),
        name="mix_ffn",
    )(x, a1, a2, wo, g, win, wout)


def _first_of_pair(shape):
    return (_lane_iota(shape) & (HEAD_DIM // 2)) == 0


def _head_sum_sq(z, member_ref):
    w = z.shape[1]
    zz = z * z
    hi = zz.astype(BF16)
    lo = (zz - hi.astype(F32)).astype(BF16)
    member = member_ref[:w, :w]
    return (jnp.dot(hi, member, preferred_element_type=F32)
            + jnp.dot(lo, member, preferred_element_type=F32))


def _norm_rope_blocks(z, gain_ref, tabs, member_ref, dim, out_ref, first_block):
    cos, sin_signed = tabs
    r = lax.rsqrt(_head_sum_sq(z, member_ref) * (1.0 / dim) + NORM_EPS)
    for e in range(z.shape[1] // LANES):
        cols = slice(e * LANES, (e + 1) * LANES)
        y = z[:, cols] * r[:, cols] * gain_ref[...]
        y = y * cos + pltpu.roll(y, LANES // 2, 1) * sin_signed
        blk = first_block + e
        out_ref[:, blk * LANES:(blk + 1) * LANES] = y.astype(BF16)


def _load_tabs(refs):
    return tuple(r[...] for r in refs)


def _run_groups(groups):
    pending = None
    for produce, consume in groups:
        z = produce()
        if pending is not None:
            pending[1](pending[0])
        pending = (z, consume)
    pending[1](pending[0])


def _proj_even_body(x_ref, g_ref, win_ref, wuq_ref, wukv_ref,
                    gaq_ref, gak_ref, gcq_ref, gckv_ref, gbq_ref, gbk_ref,
                    fc_ref, fs_ref, mc_ref, ms_ref, one_ref, two_ref,
                    qa_ref, ka_ref, va_ref, qb_ref, kb_ref, vb_ref):
    full = _load_tabs((fc_ref, fs_ref))
    mla = _load_tabs((mc_ref, ms_ref))
    h = _rms(x_ref[...], g_ref[...]).astype(BF16)
    group = 2 * LANES
    qw = N_HEADS * LANES

    def from_x(lo, width):
        return jnp.dot(h, win_ref[:, lo:lo + width], preferred_element_type=F32)

    cqn = _rms(from_x(qw, MLA_Q_RANK), gcq_ref[...]).astype(BF16)
    ckvn = _rms(from_x(qw + MLA_Q_RANK, MLA_KV_RANK), gckv_ref[...]).astype(BF16)
    tail = from_x(qw + MLA_Q_RANK + MLA_KV_RANK, 4 * LANES)
    kr = tail[:, LANES:2 * LANES]
    kr2 = jnp.concatenate([kr, kr], axis=-1)
    va_ref[...] = tail[:, 2 * LANES:].astype(BF16)
    norm_rope = functools.partial(_norm_rope_blocks, dim=HEAD_DIM)
    mla_norm_rope = functools.partial(_norm_rope_blocks, tabs=mla, member_ref=one_ref, dim=MLA_QK)
    groups = [(lambda: tail[:, :LANES], functools.partial(
        norm_rope, gain_ref=gak_ref, tabs=full, member_ref=two_ref, out_ref=ka_ref,
        first_block=0))]
    for j in range(qw // group):
        cols = slice(j * group, (j + 1) * group)
        groups += [
            (functools.partial(from_x, j * group, group), functools.partial(
                norm_rope, gain_ref=gaq_ref, tabs=full, member_ref=one_ref, out_ref=qa_ref,
                first_block=2 * j)),
            (lambda cols=cols: jnp.dot(cqn, wuq_ref[:, cols], preferred_element_type=F32),
             functools.partial(mla_norm_rope, gain_ref=gbq_ref, out_ref=qb_ref,
                               first_block=2 * j)),
            (lambda cols=cols: jnp.dot(ckvn, wukv_ref[:, cols],
                                       preferred_element_type=F32) + kr2,
             functools.partial(mla_norm_rope, gain_ref=gbk_ref, out_ref=kb_ref,
                               first_block=2 * j))]
    _run_groups(groups)
    vb_ref[...] = jnp.dot(ckvn, wukv_ref[:, qw:], preferred_element_type=F32).astype(BF16)


def _proj_odd_body(x_ref, g_ref, win_ref,
                   gcq_ref, gck_ref, gdq_ref, gdk_ref,
                   fc_ref, fs_ref, xc_ref, xs_ref, one_ref, two_ref,
                   qc_ref, kc_ref, vc_ref, qd_ref, kd_ref, vd_ref):
    full = _load_tabs((fc_ref, fs_ref))
    axial = _load_tabs((xc_ref, xs_ref))
    h = _rms(x_ref[...], g_ref[...]).astype(BF16)
    group = 2 * LANES
    wc = DIFF_HEADS * LANES
    qw = N_HEADS * LANES

    def from_x(lo, width):
        return jnp.dot(h, win_ref[:, lo:lo + width], preferred_element_type=F32)

    norm_rope = functools.partial(_norm_rope_blocks, dim=HEAD_DIM)
    groups = []
    for j in range(wc // group):
        groups += [
            (functools.partial(from_x, j * group, group), functools.partial(
                norm_rope, gain_ref=gcq_ref, tabs=full, member_ref=two_ref, out_ref=qc_ref,
                first_block=2 * j)),
            (functools.partial(from_x, wc + j * group, group), functools.partial(
                norm_rope, gain_ref=gck_ref, tabs=full, member_ref=two_ref, out_ref=kc_ref,
                first_block=2 * j))]
    for j in range(qw // group):
        groups.append((functools.partial(from_x, 2 * wc + j * group, group), functools.partial(
            norm_rope, gain_ref=gdq_ref, tabs=axial, member_ref=one_ref, out_ref=qd_ref,
            first_block=2 * j)))

    def finish_tail(tail):
        _norm_rope_blocks(tail[:, :LANES], gdk_ref, axial, two_ref, HEAD_DIM, kd_ref, 0)
        vd_ref[...] = tail[:, LANES:3 * LANES].astype(BF16)
        vc_ref[...] = tail[:, 3 * LANES:].astype(BF16)

    groups.append((functools.partial(from_x, 2 * wc + qw, 3 * LANES + wc), finish_tail))
    _run_groups(groups)


def _head_member_matrices():
    lane = jnp.arange(2 * LANES)
    same_block = (lane[:, None] // LANES) == (lane[None, :] // LANES)
    half = HEAD_DIM // 2
    same_of_pair = (lane[:, None] & half) == (lane[None, :] & half)
    return same_block.astype(BF16), (same_block & same_of_pair).astype(BF16)


def _proj(body, name, x, seq, ts, g, mats, gains, tabs, out_widths):
    n = x.shape[0]
    pos_blocks = seq // ts
    row = lambda w: pl.BlockSpec((ts, w), lambda i: (i, 0))
    tab = pl.BlockSpec((ts, LANES), lambda i: (i % pos_blocks, 0))
    members = _head_member_matrices()
    in_specs = ([row(D_MODEL), _full((1, D_MODEL))] + [_resident(m.shape) for m in mats]
                + [_full(g.shape) for g in gains] + [tab] * len(tabs)
                + [_full(m.shape) for m in members])
    return pl.pallas_call(
        body,
        grid=(n // ts,),
        in_specs=in_specs,
        out_specs=[row(w) for w in out_widths],
        out_shape=[jax.ShapeDtypeStruct((n, w), BF16) for w in out_widths],
        compiler_params=_cparams(1),
        name=name,
    )(x, g, *mats, *gains, *tabs, *members)


def _attn_pipeline_step(q, k_ref, v_ref, s_ref, m_ref):
    rows, seq = s_ref.shape

    @pl.when(pl.program_id(0) == 0)
    def _():
        s_ref[...] = jnp.zeros_like(s_ref)
        m_ref[...] = jnp.zeros_like(m_ref)

    m = m_ref[...]
    m = jnp.concatenate([m] * (KEY_TILE // LANES), axis=-1)
    ones = jnp.ones((KEY_TILE, LANES), BF16)
    acc = jnp.zeros((rows, 2 * LANES), F32)
    mx = jnp.full((rows, LANES), NEG_INF, F32)
    for j in range(seq // KEY_TILE):
        keys = slice(j * KEY_TILE, (j + 1) * KEY_TILE)
        s = lax.dot_general(q, k_ref[keys, :], (((1,), (1,)), ((), ())),
                            preferred_element_type=F32)
        p = jnp.exp2(s_ref[:, keys] - m)
        v1 = jnp.concatenate([v_ref[keys, :], ones], axis=-1)
        acc = acc + jnp.dot(p.astype(BF16), v1, preferred_element_type=F32)
        s_ref[:, keys] = s
        for blk in range(KEY_TILE // LANES):
            mx = jnp.maximum(mx, s[:, blk * LANES:(blk + 1) * LANES])
    m_ref[...] = jnp.broadcast_to(jnp.max(mx, axis=-1, keepdims=True), mx.shape)
    return acc[:, :LANES], acc[:, LANES:]


def _head_attn_body(q_ref, k_ref, v_ref, o_ref, *scratch):
    acc, l = _attn_pipeline_step(q_ref[...], k_ref, v_ref, *scratch)
    o_ref[...] = (acc / l).astype(BF16)


def _diff_attn_body(q_ref, k_ref, v_ref, lam_ref, g_ref, o_ref, *scratch, lam_init):
    tq = q_ref.shape[0]
    q = q_ref[...]
    lo = _first_of_pair(q.shape)
    zero = jnp.zeros_like(q)
    q2 = jnp.concatenate([jnp.where(lo, q, zero), jnp.where(lo, zero, q)], axis=0)
    acc, l = _attn_pipeline_step(q2, k_ref, v_ref, *scratch)
    lp = lam_ref[...]
    lam = (jnp.exp(jnp.sum(lp[0:1] * lp[1:2], keepdims=True))
           - jnp.exp(jnp.sum(lp[2:3] * lp[3:4], keepdims=True)) + lam_init)
    o = acc[:tq] / l[:tq] - lam * (acc[tq:] / l[tq:])
    o_ref[...] = (_rms(o, g_ref[...]) * (1.0 - lam_init)).astype(BF16)


def _attn_scratch(rows, seq):
    return [pltpu.VMEM((rows, seq), F32),
            pltpu.VMEM((rows, LANES), F32)]


def _unit(t, n_units, heads, qb, lag):
    u = jnp.clip(t - lag, 0, n_units - 1)
    return u // (qb * heads), (u // qb) % heads, u % qb


def _head_attn(q, k, v, batch, seq, tq, *, k_per_head, heads_per_v, name):
    n = q.shape[0]
    qb = seq // tq
    n_units = batch * N_HEADS * qb
    unit = functools.partial(_unit, n_units=n_units, heads=N_HEADS, qb=qb)

    def q_idx(t):
        b, h, i = unit(t, lag=0)
        return b * qb + i, h

    def k_idx(t):
        b, h, _ = unit(t, lag=0)
        return b, (h if k_per_head else 0)

    def v_idx(t):
        b, h, _ = unit(t, lag=1)
        return b, h // heads_per_v

    def o_idx(t):
        b, h, i = unit(t, lag=1)
        return b * qb + i, h

    return pl.pallas_call(
        _head_attn_body,
        grid=(n_units + 1,),
        in_specs=[pl.BlockSpec((tq, LANES), q_idx), pl.BlockSpec((seq, LANES), k_idx),
                  pl.BlockSpec((seq, LANES), v_idx)],
        out_specs=pl.BlockSpec((tq, LANES), o_idx),
        out_shape=jax.ShapeDtypeStruct((n, N_HEADS * LANES), BF16),
        scratch_shapes=_attn_scratch(tq, seq),
        compiler_params=_cparams(1),
        name=name,
    )(q, k, v)


def _diff_attn(q, k, v, lam_p, gain, batch, seq, tq, lam_init):
    n = q.shape[0]
    qb = seq // tq
    n_units = batch * DIFF_HEADS * qb
    unit = functools.partial(_unit, n_units=n_units, heads=DIFF_HEADS, qb=qb)

    def row_idx(lag):
        def idx(t):
            b, h, i = unit(t, lag=lag)
            return b * qb + i, h
        return idx

    def seq_idx(lag):
        def idx(t):
            b, h, _ = unit(t, lag=lag)
            return b, h
        return idx

    return pl.pallas_call(
        functools.partial(_diff_attn_body, lam_init=lam_init),
        grid=(n_units + 1,),
        in_specs=[pl.BlockSpec((tq, LANES), row_idx(0)), pl.BlockSpec((seq, LANES), seq_idx(0)),
                  pl.BlockSpec((seq, LANES), seq_idx(1)), _full(lam_p.shape), _full(gain.shape)],
        out_specs=pl.BlockSpec((tq, LANES), row_idx(1)),
        out_shape=jax.ShapeDtypeStruct((n, DIFF_HEADS * LANES), BF16),
        scratch_shapes=_attn_scratch(2 * tq, seq),
        compiler_params=_cparams(1),
        name="diff_attn",
    )(q, k, v, lam_p, gain)


def _window_attn_body(sink_ref, q_ref, kp_ref, kc_ref, kn_ref, vp_ref, vc_ref, vn_ref, o_ref,
                      *, seq):
    tq = q_ref.shape[0]
    i = pl.program_id(1)
    k = jnp.concatenate([kp_ref[...], kc_ref[...], kn_ref[...]], axis=0)
    v = jnp.concatenate([vp_ref[...], vc_ref[...], vn_ref[...]], axis=0)
    nk = tq + 2 * WINDOW
    qpos = i * tq + lax.broadcasted_iota(jnp.int32, (tq, nk), 0)
    kpos = i * tq - WINDOW + lax.broadcasted_iota(jnp.int32, (tq, nk), 1)
    valid = (jnp.abs(kpos - qpos) <= WINDOW) & (kpos >= 0) & (kpos < seq)
    heads_per_kv = N_HEADS // N_KV_HEADS
    outs = []
    for hd in range(N_HEADS):
        q = q_ref[:, hd * LANES:(hd + 1) * LANES]
        s = lax.dot_general(q, k, (((1,), (1,)), ((), ())), preferred_element_type=F32)
        s = jnp.where(valid, s, NEG_INF)
        sink = sink_ref[hd] * LOG2E
        m = jnp.maximum(jnp.max(s, axis=-1, keepdims=True), sink)
        p = jnp.exp2(s - m)
        den = jnp.sum(p, axis=-1, keepdims=True) + jnp.exp2(sink - m)
        pv = jnp.dot(p.astype(BF16), v, preferred_element_type=F32)
        kv = hd // heads_per_kv
        outs.append(pv[:, kv * LANES:(kv + 1) * LANES] / den)
    lo = _lane_iota(outs[0].shape) < HEAD_DIM
    for j in range(N_HEADS // 2):
        o_ref[:, j * LANES:(j + 1) * LANES] = jnp.where(lo, outs[2 * j], outs[2 * j + 1]).astype(BF16)


def _window_attn(sink, q, k, v, batch, seq, tq):
    n = q.shape[0]
    qb = seq // tq
    r = tq // WINDOW
    last = n // WINDOW - 1
    prev = lambda b, i: (jnp.maximum((b * qb + i) * r - 1, 0), 0)
    cur = lambda b, i: (b * qb + i, 0)
    nxt = lambda b, i: (jnp.minimum((b * qb + i + 1) * r, last), 0)
    kw, vw = k.shape[1], v.shape[1]
    return pl.pallas_call(
        functools.partial(_window_attn_body, seq=seq),
        grid=(batch, qb),
        in_specs=[pl.BlockSpec(memory_space=pltpu.SMEM),
                  pl.BlockSpec((tq, N_HEADS * LANES), cur),
                  pl.BlockSpec((WINDOW, kw), prev), pl.BlockSpec((tq, kw), cur),
                  pl.BlockSpec((WINDOW, kw), nxt),
                  pl.BlockSpec((WINDOW, vw), prev), pl.BlockSpec((tq, vw), cur),
                  pl.BlockSpec((WINDOW, vw), nxt)],
        out_specs=pl.BlockSpec((tq, N_HEADS // 2 * LANES), cur),
        out_shape=jax.ShapeDtypeStruct((n, N_HEADS // 2 * LANES), BF16),
        compiler_params=_cparams(2),
        name="window_attn",
    )(sink, q, k, k, k, v, v, v)


def _pad_last(a, width):
    return jnp.pad(a, [(0, 0)] * (a.ndim - 1) + [(0, width - a.shape[-1])])


def _pad_first(a, width):
    return jnp.pad(a, [(0, 0)] * (a.ndim - 1) + [(width - a.shape[-1], 0)])


def _pair_layout(w, axial=False):
    lead = w.shape[:-1]
    if axial:
        w = w.reshape(*lead, -1, 2, 2, 2, HEAD_DIM // 4)
        w = jnp.moveaxis(w, -2, -4)
    else:
        w = w.reshape(*lead, -1, 2, 2, HEAD_DIM // 2)
        w = jnp.swapaxes(w, -2, -3)
    return w.reshape(*lead, -1)


def _q_in_kv_lanes(w, axial=False):
    d = w.shape[0]
    w = w.reshape(d, N_KV_HEADS, N_HEADS // N_KV_HEADS, HEAD_DIM)
    z = jnp.zeros_like(w[:, 0])
    blocks = [jnp.concatenate([w[:, 0], z], -1), jnp.concatenate([z, w[:, 1]], -1)]
    return _pair_layout(jnp.stack(blocks, 1).reshape(d, N_HEADS * LANES), axial)


def _mla_layout(w):
    nope, x1, x2 = w[..., :MLA_NOPE], w[..., MLA_NOPE:MLA_NOPE + 16], w[..., MLA_NOPE + 16:]
    z = jnp.zeros(w.shape[:-1] + (LANES - MLA_QK,), w.dtype)
    return jnp.concatenate([x1, nope[..., :48], x2, nope[..., 48:], z], -1)


def _dup_v(w):
    d = w.shape[0]
    w = w.reshape(d, N_KV_HEADS, 1, HEAD_DIM)
    return jnp.broadcast_to(w, (d, N_KV_HEADS, 2, HEAD_DIM)).reshape(d, 2 * LANES)


def _angles(pos, dim):
    inv = ROPE_THETA ** (-(jnp.arange(0, dim, 2, dtype=F32) / dim))
    ang = pos.astype(F32)[:, None] * inv[None, :]
    return jnp.cos(ang), jnp.sin(ang)


def _rope_tables(seq):
    pos = jnp.arange(seq)
    c, s = _angles(pos, HEAD_DIM)
    full = (jnp.tile(c, (1, 4)), jnp.concatenate([-s, -s, s, s], -1))
    cr, sr = _angles(pos // GRID_W, HEAD_DIM // 2)
    cc, sc = _angles(pos % GRID_W, HEAD_DIM // 2)
    axial = (jnp.tile(jnp.concatenate([cr, cc], -1), (1, 4)),
             jnp.concatenate([-sr, -sc, -sr, -sc, sr, sc, sr, sc], -1))
    cm, sm = _angles(pos, MLA_ROPE)
    one, zero = jnp.ones((seq, 48), F32), jnp.zeros((seq, 48), F32)
    mla = (jnp.concatenate([cm, one, cm, one], -1), jnp.concatenate([-sm, zero, sm, zero], -1))
    return full, axial, mla


def _row(v):
    return v.astype(F32).reshape(1, -1)


def _even_params(p, i):
    w = p['ev_w_in'][i]
    sizes = (N_HEADS * HEAD_DIM, N_KV_HEADS * HEAD_DIM, N_KV_HEADS * HEAD_DIM,
             MLA_Q_RANK, MLA_KV_RANK, MLA_ROPE)
    offs = [0]
    for s in sizes:
        offs.append(offs[-1] + s)
    a_q, a_k, a_v, b_cq, b_ckv, b_kr = (w[:, offs[j]:offs[j + 1]] for j in range(6))
    kr_blk = _mla_layout(_pad_first(b_kr, MLA_QK))
    win = jnp.concatenate([_q_in_kv_lanes(a_q), b_cq, b_ckv, _pair_layout(a_k), kr_blk,
                           _dup_v(a_v)], -1)
    wuq = _mla_layout(p['b_w_uq'][i].reshape(MLA_Q_RANK, N_HEADS, MLA_QK))
    wuq = wuq.reshape(MLA_Q_RANK, N_HEADS * LANES)
    ukv = p['b_w_ukv'][i].reshape(MLA_KV_RANK, N_HEADS, MLA_NOPE + HEAD_DIM)
    k_nope = _mla_layout(_pad_last(ukv[..., :MLA_NOPE], MLA_QK))
    k_nope = k_nope.reshape(MLA_KV_RANK, N_HEADS * LANES)
    v_b = ukv[..., MLA_NOPE:].reshape(MLA_KV_RANK, N_HEADS * HEAD_DIM)
    wukv = jnp.concatenate([k_nope, v_b], -1)
    mats = [m.astype(BF16) for m in (win, wuq, wukv)]
    a_scale = HEAD_DIM ** -0.5 * LOG2E
    b_scale = MLA_QK ** -0.5 * LOG2E
    pair_gain = lambda g, scale=1.0: _row(_pair_layout(jnp.tile(g, 2)) * scale)
    gains = [pair_gain(p['a_q_norm'][i], a_scale), pair_gain(p['a_k_norm'][i]),
             _row(p['b_cq_norm'][i]), _row(p['b_ckv_norm'][i]),
             _row(_mla_layout(p['b_q_norm'][i]) * b_scale), _row(_mla_layout(p['b_k_norm'][i]))]
    return _row(p['ev_norm'][i]), mats, gains


def _odd_params(p, i):
    w = p['od_w_in'][i]
    wc = DIFF_HEADS * 2 * HEAD_DIM
    c_q, c_k, c_v = w[:, :wc], w[:, wc:2 * wc], w[:, 2 * wc:3 * wc]
    o = 3 * wc
    d_q = w[:, o:o + N_HEADS * HEAD_DIM]; o += N_HEADS * HEAD_DIM
    d_k = w[:, o:o + N_KV_HEADS * HEAD_DIM]; o += N_KV_HEADS * HEAD_DIM
    d_v = w[:, o:]
    win = jnp.concatenate([_pair_layout(c_q), _pair_layout(c_k),
                           _q_in_kv_lanes(d_q, axial=True), _pair_layout(d_k, axial=True),
                           _dup_v(d_v), c_v], -1)
    scale = HEAD_DIM ** -0.5 * LOG2E
    pair_gain = lambda g, axial, scale=1.0: _row(_pair_layout(jnp.tile(g, 2), axial) * scale)
    gains = [pair_gain(p['c_q_norm'][i], False, scale), pair_gain(p['c_k_norm'][i], False),
             pair_gain(p['d_q_norm'][i], True, scale), pair_gain(p['d_k_norm'][i], True)]
    return _row(p['od_norm'][i]), [win.astype(BF16)], gains


def _tiles(batch, seq):
    n = batch * seq
    rows = min(ATTN_SCORES // seq, seq)
    return dict(tm=min(512, n), ts=min(512, seq), tq=rows, tq_diff=rows // 2,
                tq_win=min(256, seq))


def _wo_for_padded_heads(w_mix, second_half_lanes):
    w = w_mix.reshape(N_HEADS, HEAD_DIM, D_MODEL)
    z = jnp.zeros_like(w)
    lo, hi = jnp.concatenate([w, z], 1), jnp.concatenate([z, w], 1)
    if second_half_lanes is None:
        return lo.reshape(N_HEADS * LANES, D_MODEL)
    sel = jnp.asarray(second_half_lanes).reshape(N_HEADS, 1, 1)
    return jnp.where(sel, hi, lo).reshape(N_HEADS * LANES, D_MODEL)


def _trunk(x, p, depth, tables):
    batch, seq, _ = x.shape
    t = _tiles(batch, seq)
    x = x.reshape(batch * seq, D_MODEL)
    full, axial, mla = tables
    bf = lambda a: a.astype(BF16)
    for l in range(depth):
        i = l // 2
        x = _ffn(x, _row(p['ffn1_norm'][l]), bf(p['ffn1_w_in'][l]), bf(p['ffn1_w_out'][l]), t['tm'])
        if l % 2 == 0:
            g, mats, gains = _even_params(p, i)
            qa, ka, va, qb, kb, vb = _proj(
                _proj_even_body, "proj_even", x, seq, t['ts'], g, mats, gains,
                list(full) + list(mla),
                [N_HEADS * LANES, LANES, 2 * LANES, N_HEADS * LANES, N_HEADS * LANES,
                 N_HEADS * HEAD_DIM])
            o1 = _window_attn(p['a_sink'][i].astype(F32), qa, ka, va, batch, seq, t['tq_win'])
            o2 = _head_attn(qb, kb, vb, batch, seq, t['tq'], k_per_head=True,
                            heads_per_v=2, name="mla_attn")
            w = p['ev_w_out'][i]
            half = N_HEADS * HEAD_DIM
            odd_head = [h % 2 == 1 for h in range(N_HEADS)]
            wo = bf(jnp.concatenate([w[:half], _wo_for_padded_heads(w[half:], odd_head)], 0))
        else:
            g, mats, gains = _odd_params(p, i)
            qc, kc, vc, qd, kd, vd = _proj(
                _proj_odd_body, "proj_odd", x, seq, t['ts'], g, mats, gains,
                list(full) + list(axial),
                [DIFF_HEADS * LANES] * 3 + [N_HEADS * LANES, LANES, 2 * LANES])
            lam_init = 0.8 - 0.6 * math.exp(-0.3 * l)
            o1 = _diff_attn(qc, kc, vc, p['c_lambda'][i].astype(F32), _row(p['c_out_norm'][i]),
                            batch, seq, t['tq_diff'], lam_init)
            o2 = _head_attn(qd, kd, vd, batch, seq, t['tq'], k_per_head=False,
                            heads_per_v=N_HEADS // N_KV_HEADS, name="axial_attn")
            w = p['od_w_out'][i]
            half = DIFF_HEADS * 2 * HEAD_DIM
            wo = bf(jnp.concatenate([w[:half], _wo_for_padded_heads(w[half:], None)], 0))
        x = _mix_ffn(x, o1, o2, wo, _row(p['ffn2_norm'][l]), bf(p['ffn2_w_in'][l]),
                     bf(p['ffn2_w_out'][l]), t['tm'])
    return x.reshape(batch, seq, D_MODEL)


def kernel(x_prompt, x_sample, ffn1_norm, ffn1_w_in, ffn1_w_out, ffn2_norm, ffn2_w_in, ffn2_w_out, ev_norm, ev_w_in, a_q_norm, a_k_norm, a_sink, b_cq_norm, b_w_uq, b_ckv_norm, b_w_ukv, b_q_norm, b_k_norm, ev_w_out, od_norm, od_w_in, c_q_norm, c_k_norm, c_lambda, c_out_norm, d_q_norm, d_k_norm, od_w_out):
    p = dict(ffn1_norm=ffn1_norm, ffn1_w_in=ffn1_w_in, ffn1_w_out=ffn1_w_out,
             ffn2_norm=ffn2_norm, ffn2_w_in=ffn2_w_in, ffn2_w_out=ffn2_w_out,
             ev_norm=ev_norm, ev_w_in=ev_w_in, a_q_norm=a_q_norm, a_k_norm=a_k_norm,
             a_sink=a_sink, b_cq_norm=b_cq_norm, b_w_uq=b_w_uq, b_ckv_norm=b_ckv_norm,
             b_w_ukv=b_w_ukv, b_q_norm=b_q_norm, b_k_norm=b_k_norm, ev_w_out=ev_w_out,
             od_norm=od_norm, od_w_in=od_w_in, c_q_norm=c_q_norm, c_k_norm=c_k_norm,
             c_lambda=c_lambda, c_out_norm=c_out_norm, d_q_norm=d_q_norm, d_k_norm=d_k_norm,
             od_w_out=od_w_out)
    depth = ffn1_norm.shape[0]
    tables = _rope_tables(max(x_prompt.shape[1], x_sample.shape[1]))
    return (_trunk(x_prompt, p, depth, tables), _trunk(x_sample, p, depth, tables))
```

```python
import functools
import math

import jax
import jax.numpy as jnp
from jax import lax
from jax.experimental import pallas as pl
from jax.experimental.pallas import tpu as pltpu

F32 = jnp.float32
BF16 = jnp.bfloat16

D_MODEL = 1024
D_FF = 2816
HEAD_DIM = 64
N_HEADS = 8
N_KV_HEADS = 2
WINDOW = 128
MLA_Q_RANK = 512
MLA_KV_RANK = 256
MLA_NOPE = 64
MLA_ROPE = 32
MLA_QK = MLA_NOPE + MLA_ROPE
DIFF_HEADS = 4
GRID_W = 64
ROPE_THETA = 10000.0
NORM_EPS = 1e-6
NEG_INF = -1e30
LOG2E = math.log2(math.e)
LANES = 128

VMEM_LIMIT = 56 * 1024 * 1024
ATTN_SCORES = 1024 * 8192
KEY_TILE = 256


def _cparams(n_axes):
    return pltpu.CompilerParams(
        dimension_semantics=("arbitrary",) * n_axes, vmem_limit_bytes=VMEM_LIMIT)


def _full(shape):
    return pl.BlockSpec(shape, lambda *_: (0,) * len(shape))


def _resident(shape):
    return pl.BlockSpec(shape, lambda *_: (0,) * len(shape), pipeline_mode=pl.Buffered(1))


def _resident_layer(stacked, layer):
    rest = stacked.shape[1:]
    return pl.BlockSpec((None,) + rest, lambda *_: (layer,) + (0,) * len(rest),
                        pipeline_mode=pl.Buffered(1))


def _rms(x, g):
    ms = jnp.mean(x * x, axis=-1, keepdims=True)
    return x * lax.rsqrt(ms + NORM_EPS) * g


def _lane_iota(shape):
    return lax.broadcasted_iota(jnp.int32, shape, len(shape) - 1)


def _swiglu_half_step(x, g_ref, win_ref, wout_ref):
    n = _rms(x, g_ref[...]).astype(BF16)
    z = jnp.dot(n, win_ref[...], preferred_element_type=F32)
    gate = z[:, :D_FF]
    up = z[:, D_FF:]
    h = (gate * jax.nn.sigmoid(gate) * up).astype(BF16)
    y = jnp.dot(h, wout_ref[...], preferred_element_type=F32)
    return x + 0.5 * y


def _ffn_body(x_ref, g_ref, win_ref, wout_ref, o_ref):
    o_ref[...] = _swiglu_half_step(x_ref[...], g_ref, win_ref, wout_ref)


def _mix_ffn_body(x_ref, a1_ref, a2_ref, wo_ref, g_ref, win_ref, wout_ref, o_ref):
    half = a1_ref.shape[1]
    x = x_ref[...]
    x = x + jnp.dot(a1_ref[...], wo_ref[:half, :], preferred_element_type=F32)
    x = x + jnp.dot(a2_ref[...], wo_ref[half:, :], preferred_element_type=F32)
    o_ref[...] = _swiglu_half_step(x, g_ref, win_ref, wout_ref)


def _ffn(x, g, win, wout, layer, tm):
    n = x.shape[0]
    row = pl.BlockSpec((tm, D_MODEL), lambda i: (i, 0))
    return pl.pallas_call(
        _ffn_body,
        grid=(n // tm,),
        in_specs=[row, _full((1, D_MODEL)), _resident_layer(win, layer),
                  _resident_layer(wout, layer)],
        out_specs=row,
        out_shape=jax.ShapeDtypeStruct(x.shape, F32),
        compiler_params=_cparams(1),
        name="ffn",
    )(x, g, win, wout)


def _mix_ffn(x, a1, a2, wo, g, win, wout, layer, tm):
    n = x.shape[0]
    row = pl.BlockSpec((tm, D_MODEL), lambda i: (i, 0))
    arow = lambda a: pl.BlockSpec((tm, a.shape[1]), lambda i: (i, 0))
    return pl.pallas_call(
        _mix_ffn_body,
        grid=(n // tm,),
        in_specs=[row, arow(a1), arow(a2), _resident(wo.shape), _full((1, D_MODEL)),
                  _resident_layer(win, layer), _resident_layer(wout, layer)],
        out_specs=row,
        out_shape=jax.ShapeDtypeStruct(x.shape, F32),
        compiler_params=_cparams(1),
        name="mix_ffn",
    )(x, a1, a2, wo, g, win, wout)


def _first_of_pair(shape):
    return (_lane_iota(shape) & (HEAD_DIM // 2)) == 0


def _head_sum_sq(z, member_ref):
    w = z.shape[1]
    zz = z * z
    hi = zz.astype(BF16)
    lo = (zz - hi.astype(F32)).astype(BF16)
    member = member_ref[:w, :w]
    return (jnp.dot(hi, member, preferred_element_type=F32)
            + jnp.dot(lo, member, preferred_element_type=F32))


def _norm_rope_blocks(z, gain_ref, tabs, member_ref, dim, out_ref, first_block):
    cos, sin_signed = tabs
    r = lax.rsqrt(_head_sum_sq(z, member_ref) * (1.0 / dim) + NORM_EPS)
    for e in range(z.shape[1] // LANES):
        cols = slice(e * LANES, (e + 1) * LANES)
        y = z[:, cols] * r[:, cols] * gain_ref[...]
        y = y * cos + pltpu.roll(y, LANES // 2, 1) * sin_signed
        blk = first_block + e
        out_ref[:, blk * LANES:(blk + 1) * LANES] = y.astype(BF16)


def _load_tabs(refs):
    return tuple(r[...] for r in refs)


def _run_groups(groups):
    pending = None
    for produce, consume in groups:
        z = produce()
        if pending is not None:
            pending[1](pending[0])
        pending = (z, consume)
    pending[1](pending[0])


def _proj_even_body(x_ref, g_ref, win_ref, wuq_ref, wukv_ref,
                    gaq_ref, gak_ref, gcq_ref, gckv_ref, gbq_ref, gbk_ref,
                    fc_ref, fs_ref, mc_ref, ms_ref, one_ref, two_ref,
                    qa_ref, ka_ref, va_ref, qb_ref, kb_ref, vb_ref):
    full = _load_tabs((fc_ref, fs_ref))
    mla = _load_tabs((mc_ref, ms_ref))
    h = _rms(x_ref[...], g_ref[...]).astype(BF16)
    group = 2 * LANES
    qw = N_HEADS * LANES

    def from_x(lo, width):
        return jnp.dot(h, win_ref[:, lo:lo + width], preferred_element_type=F32)

    cqn = _rms(from_x(qw, MLA_Q_RANK), gcq_ref[...]).astype(BF16)
    ckvn = _rms(from_x(qw + MLA_Q_RANK, MLA_KV_RANK), gckv_ref[...]).astype(BF16)
    tail = from_x(qw + MLA_Q_RANK + MLA_KV_RANK, 4 * LANES)
    kr = tail[:, LANES:2 * LANES]
    kr2 = jnp.concatenate([kr, kr], axis=-1)
    va_ref[...] = tail[:, 2 * LANES:].astype(BF16)
    norm_rope = functools.partial(_norm_rope_blocks, dim=HEAD_DIM)
    mla_norm_rope = functools.partial(_norm_rope_blocks, tabs=mla, member_ref=one_ref, dim=MLA_QK)
    groups = [(lambda: tail[:, :LANES], functools.partial(
        norm_rope, gain_ref=gak_ref, tabs=full, member_ref=two_ref, out_ref=ka_ref,
        first_block=0))]
    for j in range(qw // group):
        cols = slice(j * group, (j + 1) * group)
        groups += [
            (functools.partial(from_x, j * group, group), functools.partial(
                norm_rope, gain_ref=gaq_ref, tabs=full, member_ref=one_ref, out_ref=qa_ref,
                first_block=2 * j)),
            (lambda cols=cols: jnp.dot(cqn, wuq_ref[:, cols], preferred_element_type=F32),
             functools.partial(mla_norm_rope, gain_ref=gbq_ref, out_ref=qb_ref,
                               first_block=2 * j)),
            (lambda cols=cols: jnp.dot(ckvn, wukv_ref[:, cols],
                                       preferred_element_type=F32) + kr2,
             functools.partial(mla_norm_rope, gain_ref=gbk_ref, out_ref=kb_ref,
                               first_block=2 * j))]
    _run_groups(groups)
    vb_ref[...] = jnp.dot(ckvn, wukv_ref[:, qw:], preferred_element_type=F32).astype(BF16)


def _proj_odd_body(x_ref, g_ref, win_ref,
                   gcq_ref, gck_ref, gdq_ref, gdk_ref,
                   fc_ref, fs_ref, xc_ref, xs_ref, one_ref, two_ref,
                   qc_ref, kc_ref, vc_ref, qd_ref, kd_ref, vd_ref):
    full = _load_tabs((fc_ref, fs_ref))
    axial = _load_tabs((xc_ref, xs_ref))
    h = _rms(x_ref[...], g_ref[...]).astype(BF16)
    group = 2 * LANES
    wc = DIFF_HEADS * LANES
    qw = N_HEADS * LANES

    def from_x(lo, width):
        return jnp.dot(h, win_ref[:, lo:lo + width], preferred_element_type=F32)

    norm_rope = functools.partial(_norm_rope_blocks, dim=HEAD_DIM)
    groups = []
    for j in range(wc // group):
        groups += [
            (functools.partial(from_x, j * group, group), functools.partial(
                norm_rope, gain_ref=gcq_ref, tabs=full, member_ref=two_ref, out_ref=qc_ref,
                first_block=2 * j)),
            (functools.partial(from_x, wc + j * group, group), functools.partial(
                norm_rope, gain_ref=gck_ref, tabs=full, member_ref=two_ref, out_ref=kc_ref,
                first_block=2 * j))]
    for j in range(qw // group):
        groups.append((functools.partial(from_x, 2 * wc + j * group, group), functools.partial(
            norm_rope, gain_ref=gdq_ref, tabs=axial, member_ref=one_ref, out_ref=qd_ref,
            first_block=2 * j)))

    def finish_tail(tail):
        _norm_rope_blocks(tail[:, :LANES], gdk_ref, axial, two_ref, HEAD_DIM, kd_ref, 0)
        vd_ref[...] = tail[:, LANES:3 * LANES].astype(BF16)
        vc_ref[...] = tail[:, 3 * LANES:].astype(BF16)

    groups.append((functools.partial(from_x, 2 * wc + qw, 3 * LANES + wc), finish_tail))
    _run_groups(groups)


def _head_member_matrices():
    lane = jnp.arange(2 * LANES)
    same_block = (lane[:, None] // LANES) == (lane[None, :] // LANES)
    half = HEAD_DIM // 2
    same_of_pair = (lane[:, None] & half) == (lane[None, :] & half)
    return same_block.astype(BF16), (same_block & same_of_pair).astype(BF16)


def _proj(body, name, x, seq, ts, g, mats, gains, tabs, out_widths):
    n = x.shape[0]
    pos_blocks = seq // ts
    row = lambda w: pl.BlockSpec((ts, w), lambda i: (i, 0))
    tab = pl.BlockSpec((ts, LANES), lambda i: (i % pos_blocks, 0))
    members = _head_member_matrices()
    in_specs = ([row(D_MODEL), _full((1, D_MODEL))] + [_resident(m.shape) for m in mats]
                + [_full(g.shape) for g in gains] + [tab] * len(tabs)
                + [_full(m.shape) for m in members])
    return pl.pallas_call(
        body,
        grid=(n // ts,),
        in_specs=in_specs,
        out_specs=[row(w) for w in out_widths],
        out_shape=[jax.ShapeDtypeStruct((n, w), BF16) for w in out_widths],
        compiler_params=_cparams(1),
        name=name,
    )(x, g, *mats, *gains, *tabs, *members)


def _attn_pipeline_step(q, k_ref, v_ref, s_ref, m_ref):
    rows, seq = s_ref.shape

    @pl.when(pl.program_id(0) == 0)
    def _():
        s_ref[...] = jnp.zeros_like(s_ref)
        m_ref[...] = jnp.zeros_like(m_ref)

    m = m_ref[...]
    m = jnp.concatenate([m] * (KEY_TILE // LANES), axis=-1)
    ones = jnp.ones((KEY_TILE, LANES), BF16)
    acc = jnp.zeros((rows, 2 * LANES), F32)
    mx = jnp.full((rows, LANES), NEG_INF, F32)
    for j in range(seq // KEY_TILE):
        keys = slice(j * KEY_TILE, (j + 1) * KEY_TILE)
        s = lax.dot_general(q, k_ref[keys, :], (((1,), (1,)), ((), ())),
                            preferred_element_type=F32)
        p = jnp.exp2(s_ref[:, keys] - m)
        v1 = jnp.concatenate([v_ref[keys, :], ones], axis=-1)
        acc = acc + jnp.dot(p.astype(BF16), v1, preferred_element_type=F32)
        s_ref[:, keys] = s
        for blk in range(KEY_TILE // LANES):
            mx = jnp.maximum(mx, s[:, blk * LANES:(blk + 1) * LANES])
    m_ref[...] = jnp.broadcast_to(jnp.max(mx, axis=-1, keepdims=True), mx.shape)
    return acc[:, :LANES], acc[:, LANES:]


def _head_attn_body(q_ref, k_ref, v_ref, o_ref, *scratch):
    acc, l = _attn_pipeline_step(q_ref[...], k_ref, v_ref, *scratch)
    o_ref[...] = (acc / l).astype(BF16)


def _diff_attn_body(q_ref, k_ref, v_ref, lam_ref, g_ref, o_ref, *scratch, lam_init):
    tq = q_ref.shape[0]
    q = q_ref[...]
    lo = _first_of_pair(q.shape)
    zero = jnp.zeros_like(q)
    q2 = jnp.concatenate([jnp.where(lo, q, zero), jnp.where(lo, zero, q)], axis=0)
    acc, l = _attn_pipeline_step(q2, k_ref, v_ref, *scratch)
    lp = lam_ref[...]
    lam = (jnp.exp(jnp.sum(lp[0:1] * lp[1:2], keepdims=True))
           - jnp.exp(jnp.sum(lp[2:3] * lp[3:4], keepdims=True)) + lam_init)
    o = acc[:tq] / l[:tq] - lam * (acc[tq:] / l[tq:])
    o_ref[...] = (_rms(o, g_ref[...]) * (1.0 - lam_init)).astype(BF16)


def _attn_scratch(rows, seq):
    return [pltpu.VMEM((rows, seq), F32),
            pltpu.VMEM((rows, LANES), F32)]


def _unit(t, n_units, heads, qb, lag):
    u = jnp.clip(t - lag, 0, n_units - 1)
    return u // (qb * heads), (u // qb) % heads, u % qb


def _head_attn(q, k, v, batch, seq, tq, *, k_per_head, heads_per_v, name):
    n = q.shape[0]
    qb = seq // tq
    n_units = batch * N_HEADS * qb
    unit = functools.partial(_unit, n_units=n_units, heads=N_HEADS, qb=qb)

    def q_idx(t):
        b, h, i = unit(t, lag=0)
        return b * qb + i, h

    def k_idx(t):
        b, h, _ = unit(t, lag=0)
        return b, (h if k_per_head else 0)

    def v_idx(t):
        b, h, _ = unit(t, lag=1)
        return b, h // heads_per_v

    def o_idx(t):
        b, h, i = unit(t, lag=1)
        return b * qb + i, h

    return pl.pallas_call(
        _head_attn_body,
        grid=(n_units + 1,),
        in_specs=[pl.BlockSpec((tq, LANES), q_idx), pl.BlockSpec((seq, LANES), k_idx),
                  pl.BlockSpec((seq, LANES), v_idx)],
        out_specs=pl.BlockSpec((tq, LANES), o_idx),
        out_shape=jax.ShapeDtypeStruct((n, N_HEADS * LANES), BF16),
        scratch_shapes=_attn_scratch(tq, seq),
        compiler_params=_cparams(1),
        name=name,
    )(q, k, v)


def _diff_attn(q, k, v, lam_p, gain, batch, seq, tq, lam_init):
    n = q.shape[0]
    qb = seq // tq
    n_units = batch * DIFF_HEADS * qb
    unit = functools.partial(_unit, n_units=n_units, heads=DIFF_HEADS, qb=qb)

    def row_idx(lag):
        def idx(t):
            b, h, i = unit(t, lag=lag)
            return b * qb + i, h
        return idx

    def seq_idx(lag):
        def idx(t):
            b, h, _ = unit(t, lag=lag)
            return b, h
        return idx

    return pl.pallas_call(
        functools.partial(_diff_attn_body, lam_init=lam_init),
        grid=(n_units + 1,),
        in_specs=[pl.BlockSpec((tq, LANES), row_idx(0)), pl.BlockSpec((seq, LANES), seq_idx(0)),
                  pl.BlockSpec((seq, LANES), seq_idx(1)), _full(lam_p.shape), _full(gain.shape)],
        out_specs=pl.BlockSpec((tq, LANES), row_idx(1)),
        out_shape=jax.ShapeDtypeStruct((n, DIFF_HEADS * LANES), BF16),
        scratch_shapes=_attn_scratch(2 * tq, seq),
        compiler_params=_cparams(1),
        name="diff_attn",
    )(q, k, v, lam_p, gain)


def _window_attn_body(sink_ref, q_ref, kp_ref, kc_ref, kn_ref, vp_ref, vc_ref, vn_ref, o_ref,
                      *, seq):
    tq = q_ref.shape[0]
    i = pl.program_id(1)
    k = jnp.concatenate([kp_ref[...], kc_ref[...], kn_ref[...]], axis=0)
    v = jnp.concatenate([vp_ref[...], vc_ref[...], vn_ref[...]], axis=0)
    nk = tq + 2 * WINDOW
    qpos = i * tq + lax.broadcasted_iota(jnp.int32, (tq, nk), 0)
    kpos = i * tq - WINDOW + lax.broadcasted_iota(jnp.int32, (tq, nk), 1)
    valid = (jnp.abs(kpos - qpos) <= WINDOW) & (kpos >= 0) & (kpos < seq)
    heads_per_kv = N_HEADS // N_KV_HEADS
    outs = []
    for hd in range(N_HEADS):
        q = q_ref[:, hd * LANES:(hd + 1) * LANES]
        s = lax.dot_general(q, k, (((1,), (1,)), ((), ())), preferred_element_type=F32)
        s = jnp.where(valid, s, NEG_INF)
        sink = sink_ref[hd] * LOG2E
        m = jnp.maximum(jnp.max(s, axis=-1, keepdims=True), sink)
        p = jnp.exp2(s - m)
        den = jnp.sum(p, axis=-1, keepdims=True) + jnp.exp2(sink - m)
        pv = jnp.dot(p.astype(BF16), v, preferred_element_type=F32)
        kv = hd // heads_per_kv
        outs.append(pv[:, kv * LANES:(kv + 1) * LANES] / den)
    lo = _lane_iota(outs[0].shape) < HEAD_DIM
    for j in range(N_HEADS // 2):
        o_ref[:, j * LANES:(j + 1) * LANES] = jnp.where(lo, outs[2 * j], outs[2 * j + 1]).astype(BF16)


def _window_attn(sink, q, k, v, batch, seq, tq):
    n = q.shape[0]
    qb = seq // tq
    r = tq // WINDOW
    last = n // WINDOW - 1
    prev = lambda b, i: (jnp.maximum((b * qb + i) * r - 1, 0), 0)
    cur = lambda b, i: (b * qb + i, 0)
    nxt = lambda b, i: (jnp.minimum((b * qb + i + 1) * r, last), 0)
    kw, vw = k.shape[1], v.shape[1]
    return pl.pallas_call(
        functools.partial(_window_attn_body, seq=seq),
        grid=(batch, qb),
        in_specs=[pl.BlockSpec(memory_space=pltpu.SMEM),
                  pl.BlockSpec((tq, N_HEADS * LANES), cur),
                  pl.BlockSpec((WINDOW, kw), prev), pl.BlockSpec((tq, kw), cur),
                  pl.BlockSpec((WINDOW, kw), nxt),
                  pl.BlockSpec((WINDOW, vw), prev), pl.BlockSpec((tq, vw), cur),
                  pl.BlockSpec((WINDOW, vw), nxt)],
        out_specs=pl.BlockSpec((tq, N_HEADS // 2 * LANES), cur),
        out_shape=jax.ShapeDtypeStruct((n, N_HEADS // 2 * LANES), BF16),
        compiler_params=_cparams(2),
        name="window_attn",
    )(sink, q, k, k, k, v, v, v)


def _pad_last(a, width):
    return jnp.pad(a, [(0, 0)] * (a.ndim - 1) + [(0, width - a.shape[-1])])


def _pad_first(a, width):
    return jnp.pad(a, [(0, 0)] * (a.ndim - 1) + [(width - a.shape[-1], 0)])


def _pair_layout(w, axial=False):
    lead = w.shape[:-1]
    if axial:
        w = w.reshape(*lead, -1, 2, 2, 2, HEAD_DIM // 4)
        w = jnp.moveaxis(w, -2, -4)
    else:
        w = w.reshape(*lead, -1, 2, 2, HEAD_DIM // 2)
        w = jnp.swapaxes(w, -2, -3)
    return w.reshape(*lead, -1)


def _q_in_kv_lanes(w, axial=False):
    d = w.shape[0]
    w = w.reshape(d, N_KV_HEADS, N_HEADS // N_KV_HEADS, HEAD_DIM)
    z = jnp.zeros_like(w[:, 0])
    blocks = [jnp.concatenate([w[:, 0], z], -1), jnp.concatenate([z, w[:, 1]], -1)]
    return _pair_layout(jnp.stack(blocks, 1).reshape(d, N_HEADS * LANES), axial)


def _mla_layout(w):
    nope, x1, x2 = w[..., :MLA_NOPE], w[..., MLA_NOPE:MLA_NOPE + 16], w[..., MLA_NOPE + 16:]
    z = jnp.zeros(w.shape[:-1] + (LANES - MLA_QK,), w.dtype)
    return jnp.concatenate([x1, nope[..., :48], x2, nope[..., 48:], z], -1)


def _dup_v(w):
    d = w.shape[0]
    w = w.reshape(d, N_KV_HEADS, 1, HEAD_DIM)
    return jnp.broadcast_to(w, (d, N_KV_HEADS, 2, HEAD_DIM)).reshape(d, 2 * LANES)


def _angles(pos, dim):
    inv = ROPE_THETA ** (-(jnp.arange(0, dim, 2, dtype=F32) / dim))
    ang = pos.astype(F32)[:, None] * inv[None, :]
    return jnp.cos(ang), jnp.sin(ang)


def _rope_tables(seq):
    pos = jnp.arange(seq)
    (c, s), (cr, sr), (cc, sc), (cm, sm) = lax.optimization_barrier((
        _angles(pos, HEAD_DIM),
        _angles(pos // GRID_W, HEAD_DIM // 2),
        _angles(pos % GRID_W, HEAD_DIM // 2),
        _angles(pos, MLA_ROPE)))
    full = (jnp.tile(c, (1, 4)), jnp.concatenate([-s, -s, s, s], -1))
    axial = (jnp.tile(jnp.concatenate([cr, cc], -1), (1, 4)),
             jnp.concatenate([-sr, -sc, -sr, -sc, sr, sc, sr, sc], -1))
    one, zero = jnp.ones((seq, 48), F32), jnp.zeros((seq, 48), F32)
    mla = (jnp.concatenate([cm, one, cm, one], -1), jnp.concatenate([-sm, zero, sm, zero], -1))
    return full, axial, mla


def _row(v):
    return v.astype(F32).reshape(1, -1)


def _even_params(p, i):
    w = p['ev_w_in'][i].astype(BF16)
    sizes = (N_HEADS * HEAD_DIM, N_KV_HEADS * HEAD_DIM, N_KV_HEADS * HEAD_DIM,
             MLA_Q_RANK, MLA_KV_RANK, MLA_ROPE)
    offs = [0]
    for s in sizes:
        offs.append(offs[-1] + s)
    a_q, a_k, a_v, b_cq, b_ckv, b_kr = (w[:, offs[j]:offs[j + 1]] for j in range(6))
    kr_blk = _mla_layout(_pad_first(b_kr, MLA_QK))
    win = jnp.concatenate([_q_in_kv_lanes(a_q), b_cq, b_ckv, _pair_layout(a_k), kr_blk,
                           _dup_v(a_v)], -1)
    wuq = _mla_layout(p['b_w_uq'][i].astype(BF16).reshape(MLA_Q_RANK, N_HEADS, MLA_QK))
    wuq = wuq.reshape(MLA_Q_RANK, N_HEADS * LANES)
    ukv = p['b_w_ukv'][i].astype(BF16).reshape(MLA_KV_RANK, N_HEADS, MLA_NOPE + HEAD_DIM)
    k_nope = _mla_layout(_pad_last(ukv[..., :MLA_NOPE], MLA_QK))
    k_nope = k_nope.reshape(MLA_KV_RANK, N_HEADS * LANES)
    v_b = ukv[..., MLA_NOPE:].reshape(MLA_KV_RANK, N_HEADS * HEAD_DIM)
    wukv = jnp.concatenate([k_nope, v_b], -1)
    mats = [win, wuq, wukv]
    a_scale = HEAD_DIM ** -0.5 * LOG2E
    b_scale = MLA_QK ** -0.5 * LOG2E
    pair_gain = lambda g, scale=1.0: _row(_pair_layout(jnp.tile(g, 2)) * scale)
    gains = [pair_gain(p['a_q_norm'][i], a_scale), pair_gain(p['a_k_norm'][i]),
             _row(p['b_cq_norm'][i]), _row(p['b_ckv_norm'][i]),
             _row(_mla_layout(p['b_q_norm'][i]) * b_scale), _row(_mla_layout(p['b_k_norm'][i]))]
    return _row(p['ev_norm'][i]), mats, gains


def _odd_params(p, i):
    w = p['od_w_in'][i].astype(BF16)
    wc = DIFF_HEADS * 2 * HEAD_DIM
    c_q, c_k, c_v = w[:, :wc], w[:, wc:2 * wc], w[:, 2 * wc:3 * wc]
    o = 3 * wc
    d_q = w[:, o:o + N_HEADS * HEAD_DIM]; o += N_HEADS * HEAD_DIM
    d_k = w[:, o:o + N_KV_HEADS * HEAD_DIM]; o += N_KV_HEADS * HEAD_DIM
    d_v = w[:, o:]
    win = jnp.concatenate([_pair_layout(c_q), _pair_layout(c_k),
                           _q_in_kv_lanes(d_q, axial=True), _pair_layout(d_k, axial=True),
                           _dup_v(d_v), c_v], -1)
    scale = HEAD_DIM ** -0.5 * LOG2E
    pair_gain = lambda g, axial, scale=1.0: _row(_pair_layout(jnp.tile(g, 2), axial) * scale)
    gains = [pair_gain(p['c_q_norm'][i], False, scale), pair_gain(p['c_k_norm'][i], False),
             pair_gain(p['d_q_norm'][i], True, scale), pair_gain(p['d_k_norm'][i], True)]
    return _row(p['od_norm'][i]), [win], gains


def _tiles(batch, seq):
    n = batch * seq
    rows = min(ATTN_SCORES // seq, seq)
    return dict(tm=min(512, n), ts=min(512, seq), tq=rows, tq_diff=rows // 2,
                tq_win=min(256, seq))


def _wo_for_padded_heads(w_mix, second_half_lanes):
    w = w_mix.reshape(N_HEADS, HEAD_DIM, D_MODEL)
    z = jnp.zeros_like(w)
    lo, hi = jnp.concatenate([w, z], 1), jnp.concatenate([z, w], 1)
    if second_half_lanes is None:
        return lo.reshape(N_HEADS * LANES, D_MODEL)
    sel = jnp.asarray(second_half_lanes).reshape(N_HEADS, 1, 1)
    return jnp.where(sel, hi, lo).reshape(N_HEADS * LANES, D_MODEL)


def _trunk(x, p, depth, tables, ffn_weights):
    batch, seq, _ = x.shape
    t = _tiles(batch, seq)
    x = x.reshape(batch * seq, D_MODEL)
    full, axial, mla = tables
    ffn1_in, ffn1_out, ffn2_in, ffn2_out = ffn_weights
    for l in range(depth):
        i = l // 2
        x = _ffn(x, _row(p['ffn1_norm'][l]), ffn1_in, ffn1_out, l, t['tm'])
        if l % 2 == 0:
            g, mats, gains = _even_params(p, i)
            qa, ka, va, qb, kb, vb = _proj(
                _proj_even_body, "proj_even", x, seq, t['ts'], g, mats, gains,
                list(full) + list(mla),
                [N_HEADS * LANES, LANES, 2 * LANES, N_HEADS * LANES, N_HEADS * LANES,
                 N_HEADS * HEAD_DIM])
            o1 = _window_attn(p['a_sink'][i].astype(F32), qa, ka, va, batch, seq, t['tq_win'])
            o2 = _head_attn(qb, kb, vb, batch, seq, t['tq'], k_per_head=True,
                            heads_per_v=2, name="mla_attn")
            w = p['ev_w_out'][i].astype(BF16)
            half = N_HEADS * HEAD_DIM
            odd_head = [h % 2 == 1 for h in range(N_HEADS)]
            wo = jnp.concatenate([w[:half], _wo_for_padded_heads(w[half:], odd_head)], 0)
        else:
            g, mats, gains = _odd_params(p, i)
            qc, kc, vc, qd, kd, vd = _proj(
                _proj_odd_body, "proj_odd", x, seq, t['ts'], g, mats, gains,
                list(full) + list(axial),
                [DIFF_HEADS * LANES] * 3 + [N_HEADS * LANES, LANES, 2 * LANES])
            lam_init = 0.8 - 0.6 * math.exp(-0.3 * l)
            o1 = _diff_attn(qc, kc, vc, p['c_lambda'][i].astype(F32), _row(p['c_out_norm'][i]),
                            batch, seq, t['tq_diff'], lam_init)
            o2 = _head_attn(qd, kd, vd, batch, seq, t['tq'], k_per_head=False,
                            heads_per_v=N_HEADS // N_KV_HEADS, name="axial_attn")
            w = p['od_w_out'][i].astype(BF16)
            half = DIFF_HEADS * 2 * HEAD_DIM
            wo = jnp.concatenate([w[:half], _wo_for_padded_heads(w[half:], None)], 0)
        x = _mix_ffn(x, o1, o2, wo, _row(p['ffn2_norm'][l]), ffn2_in, ffn2_out, l, t['tm'])
    return x.reshape(batch, seq, D_MODEL)


def kernel(x_prompt, x_sample, ffn1_norm, ffn1_w_in, ffn1_w_out, ffn2_norm, ffn2_w_in, ffn2_w_out, ev_norm, ev_w_in, a_q_norm, a_k_norm, a_sink, b_cq_norm, b_w_uq, b_ckv_norm, b_w_ukv, b_q_norm, b_k_norm, ev_w_out, od_norm, od_w_in, c_q_norm, c_k_norm, c_lambda, c_out_norm, d_q_norm, d_k_norm, od_w_out):
    p = dict(ffn1_norm=ffn1_norm, ffn1_w_in=ffn1_w_in, ffn1_w_out=ffn1_w_out,
             ffn2_norm=ffn2_norm, ffn2_w_in=ffn2_w_in, ffn2_w_out=ffn2_w_out,
             ev_norm=ev_norm, ev_w_in=ev_w_in, a_q_norm=a_q_norm, a_k_norm=a_k_norm,
             a_sink=a_sink, b_cq_norm=b_cq_norm, b_w_uq=b_w_uq, b_ckv_norm=b_ckv_norm,
             b_w_ukv=b_w_ukv, b_q_norm=b_q_norm, b_k_norm=b_k_norm, ev_w_out=ev_w_out,
             od_norm=od_norm, od_w_in=od_w_in, c_q_norm=c_q_norm, c_k_norm=c_k_norm,
             c_lambda=c_lambda, c_out_norm=c_out_norm, d_q_norm=d_q_norm, d_k_norm=d_k_norm,
             od_w_out=od_w_out)
    depth = ffn1_norm.shape[0]
    tables = _rope_tables(max(x_prompt.shape[1], x_sample.shape[1]))
    ffn_weights = tuple(w.astype(BF16) for w in (ffn1_w_in, ffn1_w_out, ffn2_w_in, ffn2_w_out))
    return (_trunk(x_prompt, p, depth, tables, ffn_weights),
            _trunk(x_sample, p, depth, tables, ffn_weights))
```

```python
import functools
import math

import jax
import jax.numpy as jnp
from jax import lax
from jax.experimental import pallas as pl
from jax.experimental.pallas import tpu as pltpu

F32 = jnp.float32
BF16 = jnp.bfloat16

D_MODEL = 1024
D_FF = 2816
HEAD_DIM = 64
N_HEADS = 8
N_KV_HEADS = 2
WINDOW = 128
MLA_Q_RANK = 512
MLA_KV_RANK = 256
MLA_NOPE = 64
MLA_ROPE = 32
MLA_QK = MLA_NOPE + MLA_ROPE
DIFF_HEADS = 4
GRID_W = 64
ROPE_THETA = 10000.0
NORM_EPS = 1e-6
NEG_INF = -1e30
LOG2E = math.log2(math.e)
LANES = 128

VMEM_LIMIT = 56 * 1024 * 1024
ATTN_SCORES = 1024 * 8192
KEY_TILE = 256


def _cparams(n_axes):
    return pltpu.CompilerParams(
        dimension_semantics=("arbitrary",) * n_axes, vmem_limit_bytes=VMEM_LIMIT)


def _full(shape):
    return pl.BlockSpec(shape, lambda *_: (0,) * len(shape))


def _resident(shape):
    return pl.BlockSpec(shape, lambda *_: (0,) * len(shape), pipeline_mode=pl.Buffered(1))


def _resident_layer(stacked, layer):
    rest = stacked.shape[1:]
    return pl.BlockSpec((None,) + rest, lambda *_: (layer,) + (0,) * len(rest),
                        pipeline_mode=pl.Buffered(1))


def _rms(x, g):
    ms = jnp.mean(x * x, axis=-1, keepdims=True)
    return x * lax.rsqrt(ms + NORM_EPS) * g


def _lane_iota(shape):
    return lax.broadcasted_iota(jnp.int32, shape, len(shape) - 1)


def _swiglu_half_step(x, g_ref, win_ref, wout_ref):
    n = _rms(x, g_ref[...]).astype(BF16)
    z = jnp.dot(n, win_ref[...], preferred_element_type=F32)
    gate = z[:, :D_FF]
    up = z[:, D_FF:]
    h = (gate * jax.nn.sigmoid(gate) * up).astype(BF16)
    y = jnp.dot(h, wout_ref[...], preferred_element_type=F32)
    return x + 0.5 * y


def _ffn_body(x_ref, g_ref, win_ref, wout_ref, o_ref):
    o_ref[...] = _swiglu_half_step(x_ref[...], g_ref, win_ref, wout_ref)


def _mix_ffn_body(x_ref, a1_ref, a2_ref, wo_ref, g_ref, win_ref, wout_ref, o_ref):
    half = a1_ref.shape[1]
    x = x_ref[...]
    x = x + jnp.dot(a1_ref[...], wo_ref[:half, :], preferred_element_type=F32)
    x = x + jnp.dot(a2_ref[...], wo_ref[half:, :], preferred_element_type=F32)
    o_ref[...] = _swiglu_half_step(x, g_ref, win_ref, wout_ref)


def _ffn(x, g, win, wout, layer, tm):
    n = x.shape[0]
    row = pl.BlockSpec((tm, D_MODEL), lambda i: (i, 0))
    return pl.pallas_call(
        _ffn_body,
        grid=(n // tm,),
        in_specs=[row, _full((1, D_MODEL)), _resident_layer(win, layer),
                  _resident_layer(wout, layer)],
        out_specs=row,
        out_shape=jax.ShapeDtypeStruct(x.shape, F32),
        compiler_params=_cparams(1),
        name="ffn",
    )(x, g, win, wout)


def _mix_ffn(x, a1, a2, wo, g, win, wout, layer, tm):
    n = x.shape[0]
    row = pl.BlockSpec((tm, D_MODEL), lambda i: (i, 0))
    arow = lambda a: pl.BlockSpec((tm, a.shape[1]), lambda i: (i, 0))
    return pl.pallas_call(
        _mix_ffn_body,
        grid=(n // tm,),
        in_specs=[row, arow(a1), arow(a2), _resident(wo.shape), _full((1, D_MODEL)),
                  _resident_layer(win, layer), _resident_layer(wout, layer)],
        out_specs=row,
        out_shape=jax.ShapeDtypeStruct(x.shape, F32),
        compiler_params=_cparams(1),
        name="mix_ffn",
    )(x, a1, a2, wo, g, win, wout)


def _first_of_pair(shape):
    return (_lane_iota(shape) & (HEAD_DIM // 2)) == 0


def _head_sum_sq(z, member_ref):
    w = z.shape[1]
    zz = z * z
    hi = zz.astype(BF16)
    lo = (zz - hi.astype(F32)).astype(BF16)
    member = member_ref[:w, :w]
    return (jnp.dot(hi, member, preferred_element_type=F32)
            + jnp.dot(lo, member, preferred_element_type=F32))


def _norm_rope_blocks(z, gain_ref, tabs, member_ref, dim, out_ref, first_block):
    cos, sin_signed = tabs
    r = lax.rsqrt(_head_sum_sq(z, member_ref) * (1.0 / dim) + NORM_EPS)
    for e in range(z.shape[1] // LANES):
        cols = slice(e * LANES, (e + 1) * LANES)
        y = z[:, cols] * r[:, cols] * gain_ref[...]
        y = y * cos + pltpu.roll(y, LANES // 2, 1) * sin_signed
        blk = first_block + e
        out_ref[:, blk * LANES:(blk + 1) * LANES] = y.astype(BF16)


def _load_tabs(refs):
    return tuple(r[...] for r in refs)


def _run_groups(groups):
    pending = None
    for produce, consume in groups:
        z = produce()
        if pending is not None:
            pending[1](pending[0])
        pending = (z, consume)
    pending[1](pending[0])


def _proj_even_body(x_ref, g_ref, win_ref, wuq_ref, wukv_ref,
                    gaq_ref, gak_ref, gcq_ref, gckv_ref, gbq_ref, gbk_ref,
                    fc_ref, fs_ref, mc_ref, ms_ref, one_ref, two_ref,
                    qa_ref, ka_ref, va_ref, qb_ref, kb_ref, vb_ref):
    full = _load_tabs((fc_ref, fs_ref))
    mla = _load_tabs((mc_ref, ms_ref))
    h = _rms(x_ref[...], g_ref[...]).astype(BF16)
    group = 2 * LANES
    qw = N_HEADS * LANES

    def from_x(lo, width):
        return jnp.dot(h, win_ref[:, lo:lo + width], preferred_element_type=F32)

    cqn = _rms(from_x(qw, MLA_Q_RANK), gcq_ref[...]).astype(BF16)
    ckvn = _rms(from_x(qw + MLA_Q_RANK, MLA_KV_RANK), gckv_ref[...]).astype(BF16)
    tail = from_x(qw + MLA_Q_RANK + MLA_KV_RANK, 4 * LANES)
    kr = tail[:, LANES:2 * LANES]
    kr2 = jnp.concatenate([kr, kr], axis=-1)
    va_ref[...] = tail[:, 2 * LANES:].astype(BF16)
    norm_rope = functools.partial(_norm_rope_blocks, dim=HEAD_DIM)
    mla_norm_rope = functools.partial(_norm_rope_blocks, tabs=mla, member_ref=one_ref, dim=MLA_QK)
    groups = [(lambda: tail[:, :LANES], functools.partial(
        norm_rope, gain_ref=gak_ref, tabs=full, member_ref=two_ref, out_ref=ka_ref,
        first_block=0))]
    for j in range(qw // group):
        cols = slice(j * group, (j + 1) * group)
        groups += [
            (functools.partial(from_x, j * group, group), functools.partial(
                norm_rope, gain_ref=gaq_ref, tabs=full, member_ref=one_ref, out_ref=qa_ref,
                first_block=2 * j)),
            (lambda cols=cols: jnp.dot(cqn, wuq_ref[:, cols], preferred_element_type=F32),
             functools.partial(mla_norm_rope, gain_ref=gbq_ref, out_ref=qb_ref,
                               first_block=2 * j)),
            (lambda cols=cols: jnp.dot(ckvn, wukv_ref[:, cols],
                                       preferred_element_type=F32) + kr2,
             functools.partial(mla_norm_rope, gain_ref=gbk_ref, out_ref=kb_ref,
                               first_block=2 * j))]
    _run_groups(groups)
    vb_ref[...] = jnp.dot(ckvn, wukv_ref[:, qw:], preferred_element_type=F32).astype(BF16)


def _proj_odd_body(x_ref, g_ref, win_ref,
                   gcq_ref, gck_ref, gdq_ref, gdk_ref,
                   fc_ref, fs_ref, xc_ref, xs_ref, one_ref, two_ref,
                   qc_ref, kc_ref, vc_ref, qd_ref, kd_ref, vd_ref):
    full = _load_tabs((fc_ref, fs_ref))
    axial = _load_tabs((xc_ref, xs_ref))
    h = _rms(x_ref[...], g_ref[...]).astype(BF16)
    group = 2 * LANES
    wc = DIFF_HEADS * LANES
    qw = N_HEADS * LANES

    def from_x(lo, width):
        return jnp.dot(h, win_ref[:, lo:lo + width], preferred_element_type=F32)

    norm_rope = functools.partial(_norm_rope_blocks, dim=HEAD_DIM)
    groups = []
    for j in range(wc // group):
        groups += [
            (functools.partial(from_x, j * group, group), functools.partial(
                norm_rope, gain_ref=gcq_ref, tabs=full, member_ref=two_ref, out_ref=qc_ref,
                first_block=2 * j)),
            (functools.partial(from_x, wc + j * group, group), functools.partial(
                norm_rope, gain_ref=gck_ref, tabs=full, member_ref=two_ref, out_ref=kc_ref,
                first_block=2 * j))]
    for j in range(qw // group):
        groups.append((functools.partial(from_x, 2 * wc + j * group, group), functools.partial(
            norm_rope, gain_ref=gdq_ref, tabs=axial, member_ref=one_ref, out_ref=qd_ref,
            first_block=2 * j)))

    def finish_tail(tail):
        _norm_rope_blocks(tail[:, :LANES], gdk_ref, axial, two_ref, HEAD_DIM, kd_ref, 0)
        vd_ref[...] = tail[:, LANES:3 * LANES].astype(BF16)
        vc_ref[...] = tail[:, 3 * LANES:].astype(BF16)

    groups.append((functools.partial(from_x, 2 * wc + qw, 3 * LANES + wc), finish_tail))
    _run_groups(groups)


def _head_member_matrices():
    lane = jnp.arange(2 * LANES)
    same_block = (lane[:, None] // LANES) == (lane[None, :] // LANES)
    half = HEAD_DIM // 2
    same_of_pair = (lane[:, None] & half) == (lane[None, :] & half)
    return same_block.astype(BF16), (same_block & same_of_pair).astype(BF16)


def _proj(body, name, x, seq, ts, g, mats, gains, tabs, out_widths):
    n = x.shape[0]
    pos_blocks = seq // ts
    row = lambda w: pl.BlockSpec((ts, w), lambda i: (i, 0))
    tab = pl.BlockSpec((ts, LANES), lambda i: (i % pos_blocks, 0))
    members = _head_member_matrices()
    in_specs = ([row(D_MODEL), _full((1, D_MODEL))] + [_resident(m.shape) for m in mats]
                + [_full(g.shape) for g in gains] + [tab] * len(tabs)
                + [_full(m.shape) for m in members])
    return pl.pallas_call(
        body,
        grid=(n // ts,),
        in_specs=in_specs,
        out_specs=[row(w) for w in out_widths],
        out_shape=[jax.ShapeDtypeStruct((n, w), BF16) for w in out_widths],
        compiler_params=_cparams(1),
        name=name,
    )(x, g, *mats, *gains, *tabs, *members)


def _attn_pipeline_step(q, k_ref, v_ref, s_ref, m_ref):
    rows, seq = s_ref.shape

    @pl.when(pl.program_id(0) == 0)
    def _():
        s_ref[...] = jnp.zeros_like(s_ref)
        m_ref[...] = jnp.zeros_like(m_ref)

    m = m_ref[...]
    m = jnp.concatenate([m] * (KEY_TILE // LANES), axis=-1)
    ones = jnp.ones((KEY_TILE, LANES), BF16)
    acc = jnp.zeros((rows, 2 * LANES), F32)
    mx = jnp.full((rows, LANES), NEG_INF, F32)
    for j in range(seq // KEY_TILE):
        keys = slice(j * KEY_TILE, (j + 1) * KEY_TILE)
        s = lax.dot_general(q, k_ref[keys, :], (((1,), (1,)), ((), ())),
                            preferred_element_type=F32)
        p = jnp.exp2(s_ref[:, keys] - m)
        v1 = jnp.concatenate([v_ref[keys, :], ones], axis=-1)
        acc = acc + jnp.dot(p.astype(BF16), v1, preferred_element_type=F32)
        s_ref[:, keys] = s
        for blk in range(KEY_TILE // LANES):
            mx = jnp.maximum(mx, s[:, blk * LANES:(blk + 1) * LANES])
    m_ref[...] = jnp.broadcast_to(jnp.max(mx, axis=-1, keepdims=True), mx.shape)
    return acc[:, :LANES], acc[:, LANES:]


def _head_attn_body(q_ref, k_ref, v_ref, o_ref, *scratch):
    acc, l = _attn_pipeline_step(q_ref[...], k_ref, v_ref, *scratch)
    o = (acc / l).astype(BF16)
    second_of_pair = (jnp.maximum(pl.program_id(0) - 1, 0) % 2) == 1

    @pl.when(jnp.logical_not(second_of_pair))
    def _():
        o_ref[...] = o

    @pl.when(second_of_pair)
    def _():
        o_ref[:, HEAD_DIM:] = o[:, HEAD_DIM:]


def _diff_attn_body(q_ref, k_ref, v_ref, lam_ref, g_ref, o_ref, *scratch, lam_init):
    tq = q_ref.shape[0]
    q = q_ref[...]
    lo = _first_of_pair(q.shape)
    zero = jnp.zeros_like(q)
    q2 = jnp.concatenate([jnp.where(lo, q, zero), jnp.where(lo, zero, q)], axis=0)
    acc, l = _attn_pipeline_step(q2, k_ref, v_ref, *scratch)
    lp = lam_ref[...]
    lam = (jnp.exp(jnp.sum(lp[0:1] * lp[1:2], keepdims=True))
           - jnp.exp(jnp.sum(lp[2:3] * lp[3:4], keepdims=True)) + lam_init)
    o = acc[:tq] / l[:tq] - lam * (acc[tq:] / l[tq:])
    o_ref[...] = (_rms(o, g_ref[...]) * (1.0 - lam_init)).astype(BF16)


def _attn_scratch(rows, seq):
    return [pltpu.VMEM((rows, seq), F32),
            pltpu.VMEM((rows, LANES), F32)]


def _unit(t, n_units, heads, qb, lag):
    u = jnp.clip(t - lag, 0, n_units - 1)
    return u // (qb * heads), (u // qb) % heads, u % qb


def _paired_unit(t, n_units, qb, lag):
    u = jnp.clip(t - lag, 0, n_units - 1)
    pair = (u // (2 * qb)) % (N_HEADS // 2)
    return u // (qb * N_HEADS), 2 * pair + u % 2, (u // 2) % qb


def _head_attn(q, k, v, batch, seq, tq, *, k_per_head, heads_per_v, name):
    n = q.shape[0]
    qb = seq // tq
    n_units = batch * N_HEADS * qb
    unit = functools.partial(_paired_unit, n_units=n_units, qb=qb)

    def q_idx(t):
        b, h, i = unit(t, lag=0)
        return b * qb + i, h

    def k_idx(t):
        b, h, _ = unit(t, lag=0)
        return b, (h if k_per_head else 0)

    def v_idx(t):
        b, h, _ = unit(t, lag=1)
        return b, h // heads_per_v

    def o_idx(t):
        b, h, i = unit(t, lag=1)
        return b * qb + i, h // 2

    return pl.pallas_call(
        _head_attn_body,
        grid=(n_units + 1,),
        in_specs=[pl.BlockSpec((tq, LANES), q_idx), pl.BlockSpec((seq, LANES), k_idx),
                  pl.BlockSpec((seq, LANES), v_idx)],
        out_specs=pl.BlockSpec((tq, LANES), o_idx),
        out_shape=jax.ShapeDtypeStruct((n, N_HEADS * HEAD_DIM), BF16),
        scratch_shapes=_attn_scratch(tq, seq),
        compiler_params=_cparams(1),
        name=name,
    )(q, k, v)


def _diff_attn(q, k, v, lam_p, gain, batch, seq, tq, lam_init):
    n = q.shape[0]
    qb = seq // tq
    n_units = batch * DIFF_HEADS * qb
    unit = functools.partial(_unit, n_units=n_units, heads=DIFF_HEADS, qb=qb)

    def row_idx(lag):
        def idx(t):
            b, h, i = unit(t, lag=lag)
            return b * qb + i, h
        return idx

    def seq_idx(lag):
        def idx(t):
            b, h, _ = unit(t, lag=lag)
            return b, h
        return idx

    return pl.pallas_call(
        functools.partial(_diff_attn_body, lam_init=lam_init),
        grid=(n_units + 1,),
        in_specs=[pl.BlockSpec((tq, LANES), row_idx(0)), pl.BlockSpec((seq, LANES), seq_idx(0)),
                  pl.BlockSpec((seq, LANES), seq_idx(1)), _full(lam_p.shape), _full(gain.shape)],
        out_specs=pl.BlockSpec((tq, LANES), row_idx(1)),
        out_shape=jax.ShapeDtypeStruct((n, DIFF_HEADS * LANES), BF16),
        scratch_shapes=_attn_scratch(2 * tq, seq),
        compiler_params=_cparams(1),
        name="diff_attn",
    )(q, k, v, lam_p, gain)


def _window_attn_body(sink_ref, q_ref, kp_ref, kc_ref, kn_ref, vp_ref, vc_ref, vn_ref, o_ref,
                      *, seq):
    tq = q_ref.shape[0]
    i = pl.program_id(1)
    k = jnp.concatenate([kp_ref[...], kc_ref[...], kn_ref[...]], axis=0)
    v = jnp.concatenate([vp_ref[...], vc_ref[...], vn_ref[...]], axis=0)
    nk = tq + 2 * WINDOW
    qpos = i * tq + lax.broadcasted_iota(jnp.int32, (tq, nk), 0)
    kpos = i * tq - WINDOW + lax.broadcasted_iota(jnp.int32, (tq, nk), 1)
    valid = (jnp.abs(kpos - qpos) <= WINDOW) & (kpos >= 0) & (kpos < seq)
    heads_per_kv = N_HEADS // N_KV_HEADS
    outs = []
    for hd in range(N_HEADS):
        q = q_ref[:, hd * LANES:(hd + 1) * LANES]
        s = lax.dot_general(q, k, (((1,), (1,)), ((), ())), preferred_element_type=F32)
        s = jnp.where(valid, s, NEG_INF)
        sink = sink_ref[hd] * LOG2E
        m = jnp.maximum(jnp.max(s, axis=-1, keepdims=True), sink)
        p = jnp.exp2(s - m)
        den = jnp.sum(p, axis=-1, keepdims=True) + jnp.exp2(sink - m)
        pv = jnp.dot(p.astype(BF16), v, preferred_element_type=F32)
        kv = hd // heads_per_kv
        outs.append(pv[:, kv * LANES:(kv + 1) * LANES] / den)
    lo = _lane_iota(outs[0].shape) < HEAD_DIM
    for j in range(N_HEADS // 2):
        o_ref[:, j * LANES:(j + 1) * LANES] = jnp.where(lo, outs[2 * j], outs[2 * j + 1]).astype(BF16)


def _window_attn(sink, q, k, v, batch, seq, tq):
    n = q.shape[0]
    qb = seq // tq
    r = tq // WINDOW
    last = n // WINDOW - 1
    prev = lambda b, i: (jnp.maximum((b * qb + i) * r - 1, 0), 0)
    cur = lambda b, i: (b * qb + i, 0)
    nxt = lambda b, i: (jnp.minimum((b * qb + i + 1) * r, last), 0)
    kw, vw = k.shape[1], v.shape[1]
    return pl.pallas_call(
        functools.partial(_window_attn_body, seq=seq),
        grid=(batch, qb),
        in_specs=[pl.BlockSpec(memory_space=pltpu.SMEM),
                  pl.BlockSpec((tq, N_HEADS * LANES), cur),
                  pl.BlockSpec((WINDOW, kw), prev), pl.BlockSpec((tq, kw), cur),
                  pl.BlockSpec((WINDOW, kw), nxt),
                  pl.BlockSpec((WINDOW, vw), prev), pl.BlockSpec((tq, vw), cur),
                  pl.BlockSpec((WINDOW, vw), nxt)],
        out_specs=pl.BlockSpec((tq, N_HEADS // 2 * LANES), cur),
        out_shape=jax.ShapeDtypeStruct((n, N_HEADS // 2 * LANES), BF16),
        compiler_params=_cparams(2),
        name="window_attn",
    )(sink, q, k, k, k, v, v, v)


def _pad_last(a, width):
    return jnp.pad(a, [(0, 0)] * (a.ndim - 1) + [(0, width - a.shape[-1])])


def _pad_first(a, width):
    return jnp.pad(a, [(0, 0)] * (a.ndim - 1) + [(width - a.shape[-1], 0)])


def _pair_layout(w, axial=False):
    lead = w.shape[:-1]
    if axial:
        w = w.reshape(*lead, -1, 2, 2, 2, HEAD_DIM // 4)
        w = jnp.moveaxis(w, -2, -4)
    else:
        w = w.reshape(*lead, -1, 2, 2, HEAD_DIM // 2)
        w = jnp.swapaxes(w, -2, -3)
    return w.reshape(*lead, -1)


def _q_in_kv_lanes(w, axial=False):
    d = w.shape[0]
    w = w.reshape(d, N_KV_HEADS, N_HEADS // N_KV_HEADS, HEAD_DIM)
    z = jnp.zeros_like(w[:, 0])
    blocks = [jnp.concatenate([w[:, 0], z], -1), jnp.concatenate([z, w[:, 1]], -1)]
    return _pair_layout(jnp.stack(blocks, 1).reshape(d, N_HEADS * LANES), axial)


def _mla_layout(w):
    nope, x1, x2 = w[..., :MLA_NOPE], w[..., MLA_NOPE:MLA_NOPE + 16], w[..., MLA_NOPE + 16:]
    z = jnp.zeros(w.shape[:-1] + (LANES - MLA_QK,), w.dtype)
    return jnp.concatenate([x1, nope[..., :48], x2, nope[..., 48:], z], -1)


def _dup_v(w):
    d = w.shape[0]
    w = w.reshape(d, N_KV_HEADS, 1, HEAD_DIM)
    return jnp.broadcast_to(w, (d, N_KV_HEADS, 2, HEAD_DIM)).reshape(d, 2 * LANES)


def _angles(pos, dim):
    inv = ROPE_THETA ** (-(jnp.arange(0, dim, 2, dtype=F32) / dim))
    ang = pos.astype(F32)[:, None] * inv[None, :]
    return jnp.cos(ang), jnp.sin(ang)


def _rope_tables(seq):
    c, s = _angles(jnp.arange(seq), HEAD_DIM)
    full = (jnp.tile(c, (1, 4)), jnp.concatenate([-s, -s, s, s], -1))
    cm, sm = c[:, ::2], s[:, ::2]
    grid_rows = seq // GRID_W
    cr, sr = (jnp.repeat(t[:grid_rows], GRID_W, axis=0) for t in (cm, sm))
    cc, sc = (jnp.tile(t[:GRID_W], (grid_rows, 1)) for t in (cm, sm))
    axial = (jnp.tile(jnp.concatenate([cr, cc], -1), (1, 4)),
             jnp.concatenate([-sr, -sc, -sr, -sc, sr, sc, sr, sc], -1))
    one, zero = jnp.ones((seq, 48), F32), jnp.zeros((seq, 48), F32)
    mla = (jnp.concatenate([cm, one, cm, one], -1), jnp.concatenate([-sm, zero, sm, zero], -1))
    return full, axial, mla


def _row(v):
    return v.astype(F32).reshape(1, -1)


def _even_params(p, i):
    w = p['ev_w_in'][i].astype(BF16)
    sizes = (N_HEADS * HEAD_DIM, N_KV_HEADS * HEAD_DIM, N_KV_HEADS * HEAD_DIM,
             MLA_Q_RANK, MLA_KV_RANK, MLA_ROPE)
    offs = [0]
    for s in sizes:
        offs.append(offs[-1] + s)
    a_q, a_k, a_v, b_cq, b_ckv, b_kr = (w[:, offs[j]:offs[j + 1]] for j in range(6))
    kr_blk = _mla_layout(_pad_first(b_kr, MLA_QK))
    win = jnp.concatenate([_q_in_kv_lanes(a_q), b_cq, b_ckv, _pair_layout(a_k), kr_blk,
                           _dup_v(a_v)], -1)
    wuq = _mla_layout(p['b_w_uq'][i].astype(BF16).reshape(MLA_Q_RANK, N_HEADS, MLA_QK))
    wuq = wuq.reshape(MLA_Q_RANK, N_HEADS * LANES)
    ukv = p['b_w_ukv'][i].astype(BF16).reshape(MLA_KV_RANK, N_HEADS, MLA_NOPE + HEAD_DIM)
    k_nope = _mla_layout(_pad_last(ukv[..., :MLA_NOPE], MLA_QK))
    k_nope = k_nope.reshape(MLA_KV_RANK, N_HEADS * LANES)
    v_b = ukv[..., MLA_NOPE:].reshape(MLA_KV_RANK, N_HEADS * HEAD_DIM)
    wukv = jnp.concatenate([k_nope, v_b], -1)
    mats = [win, wuq, wukv]
    a_scale = HEAD_DIM ** -0.5 * LOG2E
    b_scale = MLA_QK ** -0.5 * LOG2E
    pair_gain = lambda g, scale=1.0: _row(_pair_layout(jnp.tile(g, 2)) * scale)
    gains = [pair_gain(p['a_q_norm'][i], a_scale), pair_gain(p['a_k_norm'][i]),
             _row(p['b_cq_norm'][i]), _row(p['b_ckv_norm'][i]),
             _row(_mla_layout(p['b_q_norm'][i]) * b_scale), _row(_mla_layout(p['b_k_norm'][i]))]
    return _row(p['ev_norm'][i]), mats, gains


def _odd_params(p, i):
    w = p['od_w_in'][i].astype(BF16)
    wc = DIFF_HEADS * 2 * HEAD_DIM
    c_q, c_k, c_v = w[:, :wc], w[:, wc:2 * wc], w[:, 2 * wc:3 * wc]
    o = 3 * wc
    d_q = w[:, o:o + N_HEADS * HEAD_DIM]; o += N_HEADS * HEAD_DIM
    d_k = w[:, o:o + N_KV_HEADS * HEAD_DIM]; o += N_KV_HEADS * HEAD_DIM
    d_v = w[:, o:]
    win = jnp.concatenate([_pair_layout(c_q), _pair_layout(c_k),
                           _q_in_kv_lanes(d_q, axial=True), _pair_layout(d_k, axial=True),
                           _dup_v(d_v), c_v], -1)
    scale = HEAD_DIM ** -0.5 * LOG2E
    pair_gain = lambda g, axial, scale=1.0: _row(_pair_layout(jnp.tile(g, 2), axial) * scale)
    gains = [pair_gain(p['c_q_norm'][i], False, scale), pair_gain(p['c_k_norm'][i], False),
             pair_gain(p['d_q_norm'][i], True, scale), pair_gain(p['d_k_norm'][i], True)]
    return _row(p['od_norm'][i]), [win], gains


def _tiles(batch, seq):
    n = batch * seq
    rows = min(ATTN_SCORES // seq, seq)
    return dict(tm=min(512, n), ts=min(512, seq), tq=rows, tq_diff=rows // 2,
                tq_win=min(256, seq))


def _trunk(x, p, depth, tables, ffn_weights):
    batch, seq, _ = x.shape
    t = _tiles(batch, seq)
    x = x.reshape(batch * seq, D_MODEL)
    full, axial, mla = tables
    ffn1_in, ffn1_out, ffn2_in, ffn2_out = ffn_weights
    for l in range(depth):
        i = l // 2
        x = _ffn(x, _row(p['ffn1_norm'][l]), ffn1_in, ffn1_out, l, t['tm'])
        if l % 2 == 0:
            g, mats, gains = _even_params(p, i)
            qa, ka, va, qb, kb, vb = _proj(
                _proj_even_body, "proj_even", x, seq, t['ts'], g, mats, gains,
                list(full) + list(mla),
                [N_HEADS * LANES, LANES, 2 * LANES, N_HEADS * LANES, N_HEADS * LANES,
                 N_HEADS * HEAD_DIM])
            o1 = _window_attn(p['a_sink'][i].astype(F32), qa, ka, va, batch, seq, t['tq_win'])
            o2 = _head_attn(qb, kb, vb, batch, seq, t['tq'], k_per_head=True,
                            heads_per_v=2, name="mla_attn")
            wo = p['ev_w_out'][i].astype(BF16)
        else:
            g, mats, gains = _odd_params(p, i)
            qc, kc, vc, qd, kd, vd = _proj(
                _proj_odd_body, "proj_odd", x, seq, t['ts'], g, mats, gains,
                list(full) + list(axial),
                [DIFF_HEADS * LANES] * 3 + [N_HEADS * LANES, LANES, 2 * LANES])
            lam_init = 0.8 - 0.6 * math.exp(-0.3 * l)
            o1 = _diff_attn(qc, kc, vc, p['c_lambda'][i].astype(F32), _row(p['c_out_norm'][i]),
                            batch, seq, t['tq_diff'], lam_init)
            o2 = _head_attn(qd, kd, vd, batch, seq, t['tq'], k_per_head=False,
                            heads_per_v=N_HEADS // N_KV_HEADS, name="axial_attn")
            wo = p['od_w_out'][i].astype(BF16)
        x = _mix_ffn(x, o1, o2, wo, _row(p['ffn2_norm'][l]), ffn2_in, ffn2_out, l, t['tm'])
    return x.reshape(batch, seq, D_MODEL)


def kernel(x_prompt, x_sample, ffn1_norm, ffn1_w_in, ffn1_w_out, ffn2_norm, ffn2_w_in, ffn2_w_out, ev_norm, ev_w_in, a_q_norm, a_k_norm, a_sink, b_cq_norm, b_w_uq, b_ckv_norm, b_w_ukv, b_q_norm, b_k_norm, ev_w_out, od_norm, od_w_in, c_q_norm, c_k_norm, c_lambda, c_out_norm, d_q_norm, d_k_norm, od_w_out):
    p = dict(ffn1_norm=ffn1_norm, ffn1_w_in=ffn1_w_in, ffn1_w_out=ffn1_w_out,
             ffn2_norm=ffn2_norm, ffn2_w_in=ffn2_w_in, ffn2_w_out=ffn2_w_out,
             ev_norm=ev_norm, ev_w_in=ev_w_in, a_q_norm=a_q_norm, a_k_norm=a_k_norm,
             a_sink=a_sink, b_cq_norm=b_cq_norm, b_w_uq=b_w_uq, b_ckv_norm=b_ckv_norm,
             b_w_ukv=b_w_ukv, b_q_norm=b_q_norm, b_k_norm=b_k_norm, ev_w_out=ev_w_out,
             od_norm=od_norm, od_w_in=od_w_in, c_q_norm=c_q_norm, c_k_norm=c_k_norm,
             c_lambda=c_lambda, c_out_norm=c_out_norm, d_q_norm=d_q_norm, d_k_norm=d_k_norm,
             od_w_out=od_w_out)
    depth = ffn1_norm.shape[0]
    tables = _rope_tables(max(x_prompt.shape[1], x_sample.shape[1]))
    ffn_weights = tuple(w.astype(BF16) for w in (ffn1_w_in, ffn1_w_out, ffn2_w_in, ffn2_w_out))
    return (_trunk(x_prompt, p, depth, tables, ffn_weights),
            _trunk(x_sample, p, depth, tables, ffn_weights))
```

```python
import functools
import math

import jax
import jax.numpy as jnp
from jax import lax
from jax.experimental import pallas as pl
from jax.experimental.pallas import tpu as pltpu

F32 = jnp.float32
BF16 = jnp.bfloat16

D_MODEL = 1024
D_FF = 2816
HEAD_DIM = 64
N_HEADS = 8
N_KV_HEADS = 2
WINDOW = 128
MLA_Q_RANK = 512
MLA_KV_RANK = 256
MLA_NOPE = 64
MLA_ROPE = 32
MLA_QK = MLA_NOPE + MLA_ROPE
DIFF_HEADS = 4
GRID_W = 64
ROPE_THETA = 10000.0
NORM_EPS = 1e-6
NEG_INF = -1e30
LOG2E = math.log2(math.e)
LANES = 128

VMEM_LIMIT = 56 * 1024 * 1024
ATTN_SCORES = 1024 * 8192
KEY_TILE = 256


def _cparams(n_axes):
    return pltpu.CompilerParams(
        dimension_semantics=("arbitrary",) * n_axes, vmem_limit_bytes=VMEM_LIMIT)


def _full(shape):
    return pl.BlockSpec(shape, lambda *_: (0,) * len(shape))


def _resident(shape):
    return pl.BlockSpec(shape, lambda *_: (0,) * len(shape), pipeline_mode=pl.Buffered(1))


def _resident_layer(stacked, layer):
    rest = stacked.shape[1:]
    return pl.BlockSpec((None,) + rest, lambda *_: (layer,) + (0,) * len(rest),
                        pipeline_mode=pl.Buffered(1))


def _rms(x, g):
    ms = jnp.mean(x * x, axis=-1, keepdims=True)
    return x * lax.rsqrt(ms + NORM_EPS) * g


def _lane_iota(shape):
    return lax.broadcasted_iota(jnp.int32, shape, len(shape) - 1)


def _swiglu_half_step(x, g_ref, win_ref, wout_ref):
    n = _rms(x, g_ref[...]).astype(BF16)
    z = jnp.dot(n, win_ref[...], preferred_element_type=F32)
    gate = z[:, :D_FF]
    up = z[:, D_FF:]
    h = (gate * jax.nn.sigmoid(gate) * up).astype(BF16)
    y = jnp.dot(h, wout_ref[...], preferred_element_type=F32)
    return x + 0.5 * y


def _ffn_body(x_ref, g_ref, win_ref, wout_ref, o_ref):
    o_ref[...] = _swiglu_half_step(x_ref[...], g_ref, win_ref, wout_ref)


def _mix_ffn_body(x_ref, a1_ref, a2_ref, wo_ref, g_ref, win_ref, wout_ref, o_ref):
    half = a1_ref.shape[1]
    x = x_ref[...]
    x = x + jnp.dot(a1_ref[...], wo_ref[:half, :], preferred_element_type=F32)
    x = x + jnp.dot(a2_ref[...], wo_ref[half:, :], preferred_element_type=F32)
    o_ref[...] = _swiglu_half_step(x, g_ref, win_ref, wout_ref)


def _ffn(x, g, win, wout, layer, tm):
    n = x.shape[0]
    row = pl.BlockSpec((tm, D_MODEL), lambda i: (i, 0))
    return pl.pallas_call(
        _ffn_body,
        grid=(n // tm,),
        in_specs=[row, _full((1, D_MODEL)), _resident_layer(win, layer),
                  _resident_layer(wout, layer)],
        out_specs=row,
        out_shape=jax.ShapeDtypeStruct(x.shape, F32),
        compiler_params=_cparams(1),
        name="ffn",
    )(x, g, win, wout)


def _mix_ffn(x, a1, a2, wo, g, win, wout, layer, tm):
    n = x.shape[0]
    row = pl.BlockSpec((tm, D_MODEL), lambda i: (i, 0))
    arow = lambda a: pl.BlockSpec((tm, a.shape[1]), lambda i: (i, 0))
    return pl.pallas_call(
        _mix_ffn_body,
        grid=(n // tm,),
        in_specs=[row, arow(a1), arow(a2), _resident(wo.shape), _full((1, D_MODEL)),
                  _resident_layer(win, layer), _resident_layer(wout, layer)],
        out_specs=row,
        out_shape=jax.ShapeDtypeStruct(x.shape, F32),
        compiler_params=_cparams(1),
        name="mix_ffn",
    )(x, a1, a2, wo, g, win, wout)


def _first_of_pair(shape):
    return (_lane_iota(shape) & (HEAD_DIM // 2)) == 0


def _head_sum_sq(z, member_ref):
    w = z.shape[1]
    zz = z * z
    hi = zz.astype(BF16)
    lo = (zz - hi.astype(F32)).astype(BF16)
    member = member_ref[:w, :w]
    return (jnp.dot(hi, member, preferred_element_type=F32)
            + jnp.dot(lo, member, preferred_element_type=F32))


def _norm_rope_blocks(z, gain_ref, tabs, member_ref, dim, out_ref, first_block):
    cos, sin_signed = tabs
    r = lax.rsqrt(_head_sum_sq(z, member_ref) * (1.0 / dim) + NORM_EPS)
    for e in range(z.shape[1] // LANES):
        cols = slice(e * LANES, (e + 1) * LANES)
        y = z[:, cols] * r[:, cols] * gain_ref[...]
        y = y * cos + pltpu.roll(y, LANES // 2, 1) * sin_signed
        blk = first_block + e
        out_ref[:, blk * LANES:(blk + 1) * LANES] = y.astype(BF16)


def _load_tabs(refs):
    return tuple(r[...] for r in refs)


def _run_groups(groups):
    pending = None
    for produce, consume in groups:
        z = produce()
        if pending is not None:
            pending[1](pending[0])
        pending = (z, consume)
    pending[1](pending[0])


def _proj_even_body(x_ref, g_ref, win_ref, wuq_ref, wukv_ref,
                    gaq_ref, gak_ref, gcq_ref, gckv_ref, gbq_ref, gbk_ref,
                    fc_ref, fs_ref, mc_ref, ms_ref, one_ref, two_ref,
                    qa_ref, ka_ref, va_ref, qb_ref, kb_ref, vb_ref):
    full = _load_tabs((fc_ref, fs_ref))
    mla = _load_tabs((mc_ref, ms_ref))
    h = _rms(x_ref[...], g_ref[...]).astype(BF16)
    group = 2 * LANES
    qw = N_HEADS * LANES

    def from_x(lo, width):
        return jnp.dot(h, win_ref[:, lo:lo + width], preferred_element_type=F32)

    cqn = _rms(from_x(qw, MLA_Q_RANK), gcq_ref[...]).astype(BF16)
    ckvn = _rms(from_x(qw + MLA_Q_RANK, MLA_KV_RANK), gckv_ref[...]).astype(BF16)
    tail = from_x(qw + MLA_Q_RANK + MLA_KV_RANK, 4 * LANES)
    kr = tail[:, LANES:2 * LANES]
    kr2 = jnp.concatenate([kr, kr], axis=-1)
    va_ref[...] = tail[:, 2 * LANES:].astype(BF16)
    norm_rope = functools.partial(_norm_rope_blocks, dim=HEAD_DIM)
    mla_norm_rope = functools.partial(_norm_rope_blocks, tabs=mla, member_ref=one_ref, dim=MLA_QK)
    groups = [(lambda: tail[:, :LANES], functools.partial(
        norm_rope, gain_ref=gak_ref, tabs=full, member_ref=two_ref, out_ref=ka_ref,
        first_block=0))]
    for j in range(qw // group):
        cols = slice(j * group, (j + 1) * group)
        groups += [
            (functools.partial(from_x, j * group, group), functools.partial(
                norm_rope, gain_ref=gaq_ref, tabs=full, member_ref=one_ref, out_ref=qa_ref,
                first_block=2 * j)),
            (lambda cols=cols: jnp.dot(cqn, wuq_ref[:, cols], preferred_element_type=F32),
             functools.partial(mla_norm_rope, gain_ref=gbq_ref, out_ref=qb_ref,
                               first_block=2 * j)),
            (lambda cols=cols: jnp.dot(ckvn, wukv_ref[:, cols],
                                       preferred_element_type=F32) + kr2,
             functools.partial(mla_norm_rope, gain_ref=gbk_ref, out_ref=kb_ref,
                               first_block=2 * j))]
    _run_groups(groups)
    vb_ref[...] = jnp.dot(ckvn, wukv_ref[:, qw:], preferred_element_type=F32).astype(BF16)


def _proj_odd_body(x_ref, g_ref, win_ref,
                   gcq_ref, gck_ref, gdq_ref, gdk_ref,
                   fc_ref, fs_ref, xc_ref, xs_ref, one_ref, two_ref,
                   qc_ref, kc_ref, vc_ref, qd_ref, kd_ref, vd_ref):
    full = _load_tabs((fc_ref, fs_ref))
    axial = _load_tabs((xc_ref, xs_ref))
    h = _rms(x_ref[...], g_ref[...]).astype(BF16)
    group = 2 * LANES
    wc = DIFF_HEADS * LANES
    qw = N_HEADS * LANES

    def from_x(lo, width):
        return jnp.dot(h, win_ref[:, lo:lo + width], preferred_element_type=F32)

    norm_rope = functools.partial(_norm_rope_blocks, dim=HEAD_DIM)
    groups = []
    for j in range(wc // group):
        groups += [
            (functools.partial(from_x, j * group, group), functools.partial(
                norm_rope, gain_ref=gcq_ref, tabs=full, member_ref=two_ref, out_ref=qc_ref,
                first_block=2 * j)),
            (functools.partial(from_x, wc + j * group, group), functools.partial(
                norm_rope, gain_ref=gck_ref, tabs=full, member_ref=two_ref, out_ref=kc_ref,
                first_block=2 * j))]
    for j in range(qw // group):
        groups.append((functools.partial(from_x, 2 * wc + j * group, group), functools.partial(
            norm_rope, gain_ref=gdq_ref, tabs=axial, member_ref=one_ref, out_ref=qd_ref,
            first_block=2 * j)))

    def finish_tail(tail):
        _norm_rope_blocks(tail[:, :LANES], gdk_ref, axial, two_ref, HEAD_DIM, kd_ref, 0)
        vd_ref[...] = tail[:, LANES:3 * LANES].astype(BF16)
        vc_ref[...] = tail[:, 3 * LANES:].astype(BF16)

    groups.append((functools.partial(from_x, 2 * wc + qw, 3 * LANES + wc), finish_tail))
    _run_groups(groups)


def _head_member_matrices():
    lane = jnp.arange(2 * LANES)
    same_block = (lane[:, None] // LANES) == (lane[None, :] // LANES)
    half = HEAD_DIM // 2
    same_of_pair = (lane[:, None] & half) == (lane[None, :] & half)
    return same_block.astype(BF16), (same_block & same_of_pair).astype(BF16)


def _proj(body, name, x, seq, ts, g, mats, gains, tabs, out_widths):
    n = x.shape[0]
    pos_blocks = seq // ts
    row = lambda w: pl.BlockSpec((ts, w), lambda i: (i, 0))
    tab = pl.BlockSpec((ts, LANES), lambda i: (i % pos_blocks, 0))
    members = _head_member_matrices()
    in_specs = ([row(D_MODEL), _full((1, D_MODEL))] + [_resident(m.shape) for m in mats]
                + [_full(g.shape) for g in gains] + [tab] * len(tabs)
                + [_full(m.shape) for m in members])
    return pl.pallas_call(
        body,
        grid=(n // ts,),
        in_specs=in_specs,
        out_specs=[row(w) for w in out_widths],
        out_shape=[jax.ShapeDtypeStruct((n, w), BF16) for w in out_widths],
        compiler_params=_cparams(1),
        name=name,
    )(x, g, *mats, *gains, *tabs, *members)


def _attn_pipeline_step(q, k_ref, v_ref, s_ref, m_ref):
    rows, seq = s_ref.shape

    @pl.when(pl.program_id(0) == 0)
    def _():
        s_ref[...] = jnp.zeros_like(s_ref)
        m_ref[...] = jnp.zeros_like(m_ref)

    m = m_ref[...]
    m = jnp.concatenate([m] * (KEY_TILE // LANES), axis=-1)
    ones = jnp.ones((KEY_TILE, LANES), BF16)
    acc = jnp.zeros((rows, 2 * LANES), F32)
    mx = jnp.full((rows, LANES), NEG_INF, F32)
    for j in range(seq // KEY_TILE):
        keys = slice(j * KEY_TILE, (j + 1) * KEY_TILE)
        s = lax.dot_general(q, k_ref[keys, :], (((1,), (1,)), ((), ())),
                            preferred_element_type=F32)
        p = jnp.exp2(s_ref[:, keys] - m)
        v1 = jnp.concatenate([v_ref[keys, :], ones], axis=-1)
        acc = acc + jnp.dot(p.astype(BF16), v1, preferred_element_type=F32)
        s_ref[:, keys] = s
        for blk in range(KEY_TILE // LANES):
            mx = jnp.maximum(mx, s[:, blk * LANES:(blk + 1) * LANES])
    m_ref[...] = jnp.broadcast_to(jnp.max(mx, axis=-1, keepdims=True), mx.shape)
    return acc[:, :LANES], acc[:, LANES:]


def _head_attn_body(q_ref, k_ref, v_ref, o_ref, *scratch):
    acc, l = _attn_pipeline_step(q_ref[...], k_ref, v_ref, *scratch)
    o = (acc / l).astype(BF16)
    second_of_pair = (jnp.maximum(pl.program_id(0) - 1, 0) % 2) == 1

    @pl.when(jnp.logical_not(second_of_pair))
    def _():
        o_ref[...] = o

    @pl.when(second_of_pair)
    def _():
        o_ref[:, HEAD_DIM:] = o[:, HEAD_DIM:]


def _diff_attn_body(q_ref, k_ref, v_ref, lam_ref, g_ref, o_ref, *scratch, lam_init):
    tq = q_ref.shape[0]
    q = q_ref[...]
    lo = _first_of_pair(q.shape)
    zero = jnp.zeros_like(q)
    q2 = jnp.concatenate([jnp.where(lo, q, zero), jnp.where(lo, zero, q)], axis=0)
    acc, l = _attn_pipeline_step(q2, k_ref, v_ref, *scratch)
    lp = lam_ref[...]
    lam = (jnp.exp(jnp.sum(lp[0:1] * lp[1:2], keepdims=True))
           - jnp.exp(jnp.sum(lp[2:3] * lp[3:4], keepdims=True)) + lam_init)
    o = acc[:tq] / l[:tq] - lam * (acc[tq:] / l[tq:])
    o_ref[...] = (_rms(o, g_ref[...]) * (1.0 - lam_init)).astype(BF16)


def _attn_scratch(rows, seq):
    return [pltpu.VMEM((rows, seq), F32),
            pltpu.VMEM((rows, LANES), F32)]


def _unit(t, n_units, heads, qb, lag):
    u = jnp.clip(t - lag, 0, n_units - 1)
    return u // (qb * heads), (u // qb) % heads, u % qb


def _paired_unit(t, n_units, qb, lag):
    u = jnp.clip(t - lag, 0, n_units - 1)
    pair = (u // (2 * qb)) % (N_HEADS // 2)
    return u // (qb * N_HEADS), 2 * pair + u % 2, (u // 2) % qb


def _head_attn(q, k, v, batch, seq, tq, *, k_per_head, heads_per_v, name):
    n = q.shape[0]
    qb = seq // tq
    n_units = batch * N_HEADS * qb
    unit = functools.partial(_paired_unit, n_units=n_units, qb=qb)

    def q_idx(t):
        b, h, i = unit(t, lag=0)
        return b * qb + i, h

    def k_idx(t):
        b, h, _ = unit(t, lag=0)
        return b, (h if k_per_head else 0)

    def v_idx(t):
        b, h, _ = unit(t, lag=1)
        return b, h // heads_per_v

    def o_idx(t):
        b, h, i = unit(t, lag=1)
        return b * qb + i, h // 2

    return pl.pallas_call(
        _head_attn_body,
        grid=(n_units + 1,),
        in_specs=[pl.BlockSpec((tq, LANES), q_idx), pl.BlockSpec((seq, LANES), k_idx),
                  pl.BlockSpec((seq, LANES), v_idx)],
        out_specs=pl.BlockSpec((tq, LANES), o_idx),
        out_shape=jax.ShapeDtypeStruct((n, N_HEADS * HEAD_DIM), BF16),
        scratch_shapes=_attn_scratch(tq, seq),
        compiler_params=_cparams(1),
        name=name,
    )(q, k, v)


def _diff_attn(q, k, v, lam_p, gain, batch, seq, tq, lam_init):
    n = q.shape[0]
    qb = seq // tq
    n_units = batch * DIFF_HEADS * qb
    unit = functools.partial(_unit, n_units=n_units, heads=DIFF_HEADS, qb=qb)

    def row_idx(lag):
        def idx(t):
            b, h, i = unit(t, lag=lag)
            return b * qb + i, h
        return idx

    def seq_idx(lag):
        def idx(t):
            b, h, _ = unit(t, lag=lag)
            return b, h
        return idx

    return pl.pallas_call(
        functools.partial(_diff_attn_body, lam_init=lam_init),
        grid=(n_units + 1,),
        in_specs=[pl.BlockSpec((tq, LANES), row_idx(0)), pl.BlockSpec((seq, LANES), seq_idx(0)),
                  pl.BlockSpec((seq, LANES), seq_idx(1)), _full(lam_p.shape), _full(gain.shape)],
        out_specs=pl.BlockSpec((tq, LANES), row_idx(1)),
        out_shape=jax.ShapeDtypeStruct((n, DIFF_HEADS * LANES), BF16),
        scratch_shapes=_attn_scratch(2 * tq, seq),
        compiler_params=_cparams(1),
        name="diff_attn",
    )(q, k, v, lam_p, gain)


def _window_attn_body(sink_ref, q_ref, kp_ref, kc_ref, kn_ref, vp_ref, vc_ref, vn_ref, o_ref,
                      *, seq):
    tq = q_ref.shape[0]
    i = pl.program_id(1)
    k = jnp.concatenate([kp_ref[...], kc_ref[...], kn_ref[...]], axis=0)
    v = jnp.concatenate([vp_ref[...], vc_ref[...], vn_ref[...]], axis=0)
    nk = tq + 2 * WINDOW
    qpos = i * tq + lax.broadcasted_iota(jnp.int32, (tq, nk), 0)
    kpos = i * tq - WINDOW + lax.broadcasted_iota(jnp.int32, (tq, nk), 1)
    valid = (jnp.abs(kpos - qpos) <= WINDOW) & (kpos >= 0) & (kpos < seq)
    heads_per_kv = N_HEADS // N_KV_HEADS
    outs = []
    for hd in range(N_HEADS):
        q = q_ref[:, hd * LANES:(hd + 1) * LANES]
        s = lax.dot_general(q, k, (((1,), (1,)), ((), ())), preferred_element_type=F32)
        s = jnp.where(valid, s, NEG_INF)
        sink = sink_ref[hd] * LOG2E
        m = jnp.maximum(jnp.max(s, axis=-1, keepdims=True), sink)
        p = jnp.exp2(s - m)
        den = jnp.sum(p, axis=-1, keepdims=True) + jnp.exp2(sink - m)
        pv = jnp.dot(p.astype(BF16), v, preferred_element_type=F32)
        kv = hd // heads_per_kv
        outs.append(pv[:, kv * LANES:(kv + 1) * LANES] / den)
    lo = _lane_iota(outs[0].shape) < HEAD_DIM
    for j in range(N_HEADS // 2):
        o_ref[:, j * LANES:(j + 1) * LANES] = jnp.where(lo, outs[2 * j], outs[2 * j + 1]).astype(BF16)


def _window_attn(sink, q, k, v, batch, seq, tq):
    n = q.shape[0]
    qb = seq // tq
    r = tq // WINDOW
    last = n // WINDOW - 1
    prev = lambda b, i: (jnp.maximum((b * qb + i) * r - 1, 0), 0)
    cur = lambda b, i: (b * qb + i, 0)
    nxt = lambda b, i: (jnp.minimum((b * qb + i + 1) * r, last), 0)
    kw, vw = k.shape[1], v.shape[1]
    return pl.pallas_call(
        functools.partial(_window_attn_body, seq=seq),
        grid=(batch, qb),
        in_specs=[pl.BlockSpec(memory_space=pltpu.SMEM),
                  pl.BlockSpec((tq, N_HEADS * LANES), cur),
                  pl.BlockSpec((WINDOW, kw), prev), pl.BlockSpec((tq, kw), cur),
                  pl.BlockSpec((WINDOW, kw), nxt),
                  pl.BlockSpec((WINDOW, vw), prev), pl.BlockSpec((tq, vw), cur),
                  pl.BlockSpec((WINDOW, vw), nxt)],
        out_specs=pl.BlockSpec((tq, N_HEADS // 2 * LANES), cur),
        out_shape=jax.ShapeDtypeStruct((n, N_HEADS // 2 * LANES), BF16),
        compiler_params=_cparams(2),
        name="window_attn",
    )(sink, q, k, k, k, v, v, v)


def _pad_last(a, width):
    return jnp.pad(a, [(0, 0)] * (a.ndim - 1) + [(0, width - a.shape[-1])])


def _pad_first(a, width):
    return jnp.pad(a, [(0, 0)] * (a.ndim - 1) + [(width - a.shape[-1], 0)])


def _pair_layout(w, axial=False):
    lead = w.shape[:-1]
    if axial:
        w = w.reshape(*lead, -1, 2, 2, 2, HEAD_DIM // 4)
        w = jnp.moveaxis(w, -2, -4)
    else:
        w = w.reshape(*lead, -1, 2, 2, HEAD_DIM // 2)
        w = jnp.swapaxes(w, -2, -3)
    return w.reshape(*lead, -1)


def _q_in_kv_lanes(w, axial=False):
    d = w.shape[0]
    w = w.reshape(d, N_KV_HEADS, N_HEADS // N_KV_HEADS, HEAD_DIM)
    z = jnp.zeros_like(w[:, 0])
    blocks = [jnp.concatenate([w[:, 0], z], -1), jnp.concatenate([z, w[:, 1]], -1)]
    return _pair_layout(jnp.stack(blocks, 1).reshape(d, N_HEADS * LANES), axial)


def _mla_layout(w):
    nope, x1, x2 = w[..., :MLA_NOPE], w[..., MLA_NOPE:MLA_NOPE + 16], w[..., MLA_NOPE + 16:]
    z = jnp.zeros(w.shape[:-1] + (LANES - MLA_QK,), w.dtype)
    return jnp.concatenate([x1, nope[..., :48], x2, nope[..., 48:], z], -1)


def _dup_v(w):
    d = w.shape[0]
    w = w.reshape(d, N_KV_HEADS, 1, HEAD_DIM)
    return jnp.broadcast_to(w, (d, N_KV_HEADS, 2, HEAD_DIM)).reshape(d, 2 * LANES)


def _angles(pos, dim):
    inv = ROPE_THETA ** (-(jnp.arange(0, dim, 2, dtype=F32) / dim))
    ang = pos.astype(F32)[:, None] * inv[None, :]
    return jnp.cos(ang), jnp.sin(ang)


def _rope_tables(seq):
    pos = jnp.arange(seq)
    c, s = _angles(pos, HEAD_DIM)
    full = (jnp.tile(c, (1, 4)), jnp.concatenate([-s, -s, s, s], -1))
    cm, sm = _angles(pos, MLA_ROPE)
    grid_rows = seq // GRID_W
    cr, sr = (jnp.repeat(t[:grid_rows], GRID_W, axis=0) for t in (cm, sm))
    cc, sc = (jnp.tile(t[:GRID_W], (grid_rows, 1)) for t in (cm, sm))
    axial = (jnp.tile(jnp.concatenate([cr, cc], -1), (1, 4)),
             jnp.concatenate([-sr, -sc, -sr, -sc, sr, sc, sr, sc], -1))
    one, zero = jnp.ones((seq, 48), F32), jnp.zeros((seq, 48), F32)
    mla = (jnp.concatenate([cm, one, cm, one], -1), jnp.concatenate([-sm, zero, sm, zero], -1))
    return full, axial, mla


def _row(v):
    return v.astype(F32).reshape(1, -1)


def _even_params(p, i):
    w = p['ev_w_in'][i].astype(BF16)
    sizes = (N_HEADS * HEAD_DIM, N_KV_HEADS * HEAD_DIM, N_KV_HEADS * HEAD_DIM,
             MLA_Q_RANK, MLA_KV_RANK, MLA_ROPE)
    offs = [0]
    for s in sizes:
        offs.append(offs[-1] + s)
    a_q, a_k, a_v, b_cq, b_ckv, b_kr = (w[:, offs[j]:offs[j + 1]] for j in range(6))
    kr_blk = _mla_layout(_pad_first(b_kr, MLA_QK))
    win = jnp.concatenate([_q_in_kv_lanes(a_q), b_cq, b_ckv, _pair_layout(a_k), kr_blk,
                           _dup_v(a_v)], -1)
    wuq = _mla_layout(p['b_w_uq'][i].astype(BF16).reshape(MLA_Q_RANK, N_HEADS, MLA_QK))
    wuq = wuq.reshape(MLA_Q_RANK, N_HEADS * LANES)
    ukv = p['b_w_ukv'][i].astype(BF16).reshape(MLA_KV_RANK, N_HEADS, MLA_NOPE + HEAD_DIM)
    k_nope = _mla_layout(_pad_last(ukv[..., :MLA_NOPE], MLA_QK))
    k_nope = k_nope.reshape(MLA_KV_RANK, N_HEADS * LANES)
    v_b = ukv[..., MLA_NOPE:].reshape(MLA_KV_RANK, N_HEADS * HEAD_DIM)
    wukv = jnp.concatenate([k_nope, v_b], -1)
    mats = [win, wuq, wukv]
    a_scale = HEAD_DIM ** -0.5 * LOG2E
    b_scale = MLA_QK ** -0.5 * LOG2E
    pair_gain = lambda g, scale=1.0: _row(_pair_layout(jnp.tile(g, 2)) * scale)
    gains = [pair_gain(p['a_q_norm'][i], a_scale), pair_gain(p['a_k_norm'][i]),
             _row(p['b_cq_norm'][i]), _row(p['b_ckv_norm'][i]),
             _row(_mla_layout(p['b_q_norm'][i]) * b_scale), _row(_mla_layout(p['b_k_norm'][i]))]
    return _row(p['ev_norm'][i]), mats, gains


def _odd_params(p, i):
    w = p['od_w_in'][i].astype(BF16)
    wc = DIFF_HEADS * 2 * HEAD_DIM
    c_q, c_k, c_v = w[:, :wc], w[:, wc:2 * wc], w[:, 2 * wc:3 * wc]
    o = 3 * wc
    d_q = w[:, o:o + N_HEADS * HEAD_DIM]; o += N_HEADS * HEAD_DIM
    d_k = w[:, o:o + N_KV_HEADS * HEAD_DIM]; o += N_KV_HEADS * HEAD_DIM
    d_v = w[:, o:]
    win = jnp.concatenate([_pair_layout(c_q), _pair_layout(c_k),
                           _q_in_kv_lanes(d_q, axial=True), _pair_layout(d_k, axial=True),
                           _dup_v(d_v), c_v], -1)
    scale = HEAD_DIM ** -0.5 * LOG2E
    pair_gain = lambda g, axial, scale=1.0: _row(_pair_layout(jnp.tile(g, 2), axial) * scale)
    gains = [pair_gain(p['c_q_norm'][i], False, scale), pair_gain(p['c_k_norm'][i], False),
             pair_gain(p['d_q_norm'][i], True, scale), pair_gain(p['d_k_norm'][i], True)]
    return _row(p['od_norm'][i]), [win], gains


def _tiles(batch, seq):
    n = batch * seq
    rows = min(ATTN_SCORES // seq, seq)
    return dict(tm=min(512, n), ts=min(512, seq), tq=rows, tq_diff=rows // 2,
                tq_win=min(256, seq))


def _trunk(x, p, depth, tables, ffn_weights):
    batch, seq, _ = x.shape
    t = _tiles(batch, seq)
    x = x.reshape(batch * seq, D_MODEL)
    full, axial, mla = tables
    ffn1_in, ffn1_out, ffn2_in, ffn2_out = ffn_weights
    for l in range(depth):
        i = l // 2
        x = _ffn(x, _row(p['ffn1_norm'][l]), ffn1_in, ffn1_out, l, t['tm'])
        if l % 2 == 0:
            g, mats, gains = _even_params(p, i)
            qa, ka, va, qb, kb, vb = _proj(
                _proj_even_body, "proj_even", x, seq, t['ts'], g, mats, gains,
                list(full) + list(mla),
                [N_HEADS * LANES, LANES, 2 * LANES, N_HEADS * LANES, N_HEADS * LANES,
                 N_HEADS * HEAD_DIM])
            o1 = _window_attn(p['a_sink'][i].astype(F32), qa, ka, va, batch, seq, t['tq_win'])
            o2 = _head_attn(qb, kb, vb, batch, seq, t['tq'], k_per_head=True,
                            heads_per_v=2, name="mla_attn")
            wo = p['ev_w_out'][i].astype(BF16)
        else:
            g, mats, gains = _odd_params(p, i)
            qc, kc, vc, qd, kd, vd = _proj(
                _proj_odd_body, "proj_odd", x, seq, t['ts'], g, mats, gains,
                list(full) + list(axial),
                [DIFF_HEADS * LANES] * 3 + [N_HEADS * LANES, LANES, 2 * LANES])
            lam_init = 0.8 - 0.6 * math.exp(-0.3 * l)
            o1 = _diff_attn(qc, kc, vc, p['c_lambda'][i].astype(F32), _row(p['c_out_norm'][i]),
                            batch, seq, t['tq_diff'], lam_init)
            o2 = _head_attn(qd, kd, vd, batch, seq, t['tq'], k_per_head=False,
                            heads_per_v=N_HEADS // N_KV_HEADS, name="axial_attn")
            wo = p['od_w_out'][i].astype(BF16)
        x = _mix_ffn(x, o1, o2, wo, _row(p['ffn2_norm'][l]), ffn2_in, ffn2_out, l, t['tm'])
    return x.reshape(batch, seq, D_MODEL)


def kernel(x_prompt, x_sample, ffn1_norm, ffn1_w_in, ffn1_w_out, ffn2_norm, ffn2_w_in, ffn2_w_out, ev_norm, ev_w_in, a_q_norm, a_k_norm, a_sink, b_cq_norm, b_w_uq, b_ckv_norm, b_w_ukv, b_q_norm, b_k_norm, ev_w_out, od_norm, od_w_in, c_q_norm, c_k_norm, c_lambda, c_out_norm, d_q_norm, d_k_norm, od_w_out):
    p = dict(ffn1_norm=ffn1_norm, ffn1_w_in=ffn1_w_in, ffn1_w_out=ffn1_w_out,
             ffn2_norm=ffn2_norm, ffn2_w_in=ffn2_w_in, ffn2_w_out=ffn2_w_out,
             ev_norm=ev_norm, ev_w_in=ev_w_in, a_q_norm=a_q_norm, a_k_norm=a_k_norm,
             a_sink=a_sink, b_cq_norm=b_cq_norm, b_w_uq=b_w_uq, b_ckv_norm=b_ckv_norm,
             b_w_ukv=b_w_ukv, b_q_norm=b_q_norm, b_k_norm=b_k_norm, ev_w_out=ev_w_out,
             od_norm=od_norm, od_w_in=od_w_in, c_q_norm=c_q_norm, c_k_norm=c_k_norm,
             c_lambda=c_lambda, c_out_norm=c_out_norm, d_q_norm=d_q_norm, d_k_norm=d_k_norm,
             od_w_out=od_w_out)
    depth = ffn1_norm.shape[0]
    tables = _rope_tables(max(x_prompt.shape[1], x_sample.shape[1]))
    ffn_weights = tuple(w.astype(BF16) for w in (ffn1_w_in, ffn1_w_out, ffn2_w_in, ffn2_w_out))
    return (_trunk(x_prompt, p, depth, tables, ffn_weights),
            _trunk(x_sample, p, depth, tables, ffn_weights))
```
